```python
import functools
import jax, jax.numpy as jnp
from jax import lax
import numpy as np

D_MODEL = 1024
BATCH = 2
SEQ = 8192
DEPTH = 4
DEC_BATCH = 32
DEC_SEQ = 1
PAST_LEN = 8192
PAGE_SIZE = 128

HEAD_DIM = 64
HEADS_PER_GROUP = 4
GROUPS = ((128, 1), (512, 4), (2048, 16))
N_GROUPS = len(GROUPS)
ATTN_WIDTH = N_GROUPS * HEADS_PER_GROUP * HEAD_DIM
ATTN_OUT_WIDTH = HEADS_PER_GROUP * HEAD_DIM
CONV_CHANNELS = 768
CONV_WIDTH = 3
D_FF = 4 * D_MODEL
PLE_DIM = 256
ROPE_THETA = 10000.0
BLOCK = 128
RMS_EPS = 1e-6
NEG_INF = -1e30
IN_PROJ_WIDTH = 3 * ATTN_WIDTH + 3 * CONV_CHANNELS + 2 * D_MODEL
SPLITS = [ATTN_WIDTH, 2 * ATTN_WIDTH, 3 * ATTN_WIDTH,
          3 * ATTN_WIDTH + CONV_CHANNELS, 3 * ATTN_WIDTH + 2 * CONV_CHANNELS,
          3 * ATTN_WIDTH + 3 * CONV_CHANNELS, 3 * ATTN_WIDTH + 3 * CONV_CHANNELS + D_MODEL]

kernel_name = "dilated_swa_shortconv_gated_hybrid_step"


def rmsnorm(x, g):
    x32 = x.astype(jnp.float32)
    y = x32 * lax.rsqrt(jnp.mean(x32 * x32, axis=-1, keepdims=True) + RMS_EPS) * g.astype(jnp.float32)
    return y.astype(x.dtype)


def rope(x, pos):
    half = HEAD_DIM // 2
    inv = jnp.power(ROPE_THETA, -2.0 * jnp.arange(half, dtype=jnp.float32) / HEAD_DIM)
    ang = pos.astype(jnp.float32)[:, None] * inv[None, :]
    cos = jnp.cos(ang)[:, None, :]
    sin = jnp.sin(ang)[:, None, :]
    x32 = x.astype(jnp.float32)
    x1, x2 = x32[..., :half], x32[..., half:]
    return jnp.concatenate([x1 * cos - x2 * sin, x2 * cos + x1 * sin], axis=-1).astype(x.dtype)


def dilated_group_prompt(q, k, v, window, dil):
    b, s_len, h, dh = q.shape
    n_back = window // dil
    span = dil * BLOCK
    s_pad = -(-s_len // span) * span
    nb = s_pad // span

    def to_blocks(x):
        x = jnp.pad(x, ((0, 0), (0, s_pad - s_len), (0, 0), (0, 0)))
        x = x.reshape(b, s_pad // dil, dil, h, dh).swapaxes(1, 2)
        return x.reshape(b * dil, nb, BLOCK, h, dh)

    def with_prev(x):
        prev = jnp.pad(x, ((0, 0), (1, 0), (0, 0), (0, 0), (0, 0)))[:, :-1]
        return jnp.concatenate([prev, x], axis=2)

    qb = to_blocks(q)
    kc = with_prev(to_blocks(k))
    vc = with_prev(to_blocks(v))
    scores = jnp.einsum('gnqhd,gnkhd->gnhqk', qb, kc).astype(jnp.float32) * (HEAD_DIM ** -0.5)
    qi = jnp.arange(BLOCK)[:, None]
    ki = jnp.arange(2 * BLOCK)[None, :]
    dist = qi + BLOCK - ki
    blk = jnp.arange(nb)[:, None, None]
    valid = (dist >= 0) & (dist <= n_back) & ((blk > 0) | (ki >= BLOCK))
    scores = jnp.where(valid[None, :, None], scores, NEG_INF)
    m = jnp.max(scores, axis=-1)
    p = jnp.exp(scores - m[..., None])
    l = jnp.sum(p, axis=-1)
    o = jnp.einsum('gnhqk,gnkhd->gnqhd', p, vc.astype(jnp.float32)) / l.swapaxes(2, 3)[..., None]

    def from_blocks(x):
        tail = x.shape[3:]
        x = x.reshape(b, dil, s_pad // dil, *tail).swapaxes(1, 2).reshape(b, s_pad, *tail)
        return x[:, :s_len]

    return from_blocks(o), from_blocks(m.swapaxes(2, 3)), from_blocks(l.swapaxes(2, 3))


def dilated_group_sample(q, k_new, v_new, k_buf, v_buf, window, dil):
    t = q.shape[1]
    lb = k_buf.shape[1]
    n_back = window // dil
    kc = jnp.concatenate([k_buf, k_new.astype(k_buf.dtype)], axis=1)
    vc = jnp.concatenate([v_buf, v_new.astype(v_buf.dtype)], axis=1)
    idx = lb + jnp.arange(t)[:, None] - dil * jnp.arange(n_back + 1)[None, :]
    valid = idx >= 0
    idx = jnp.maximum(idx, 0)
    kg = kc[:, idx]
    vg = vc[:, idx]
    scores = jnp.einsum('bthd,btjhd->bhtj', q, kg).astype(jnp.float32) * (HEAD_DIM ** -0.5)
    scores = jnp.where(valid[None, None], scores, NEG_INF)
    m = jnp.max(scores, axis=-1)
    p = jnp.exp(scores - m[..., None])
    l = jnp.sum(p, axis=-1)
    o = jnp.einsum('bhtj,btjhd->bthd', p, vg.astype(jnp.float32)) / l.swapaxes(1, 2)[..., None]
    return o, m.swapaxes(1, 2), l.swapaxes(1, 2), kc[:, -lb:], vc[:, -lb:]


def merge_groups(outs, maxes, denoms, dtype):
    mx = functools.reduce(jnp.maximum, maxes)
    ws = [l * jnp.exp(m - mx) for m, l in zip(maxes, denoms)]
    num = sum(w[..., None] * o for w, o in zip(ws, outs))
    o = num / sum(ws)[..., None]
    return o.reshape(o.shape[0], o.shape[1], ATTN_OUT_WIDTH).astype(dtype)


def attn_prompt(q, k, v):
    outs, maxes, denoms, state = [], [], [], []
    for g, (window, dil) in enumerate(GROUPS):
        sl = slice(g * HEADS_PER_GROUP, (g + 1) * HEADS_PER_GROUP)
        o, m, l = dilated_group_prompt(q[:, :, sl], k[:, :, sl], v[:, :, sl], window, dil)
        outs.append(o); maxes.append(m); denoms.append(l)
        keep = min(window, q.shape[1])
        state += [k[:, -keep:, sl], v[:, -keep:, sl]]
    return merge_groups(outs, maxes, denoms, q.dtype), state


def attn_sample(q, k, v, bufs):
    outs, maxes, denoms, state = [], [], [], []
    for g, (window, dil) in enumerate(GROUPS):
        sl = slice(g * HEADS_PER_GROUP, (g + 1) * HEADS_PER_GROUP)
        o, m, l, nk, nv = dilated_group_sample(q[:, :, sl], k[:, :, sl], v[:, :, sl],
                                               bufs[2 * g], bufs[2 * g + 1], window, dil)
        outs.append(o); maxes.append(m); denoms.append(l)
        state += [nk, nv]
    return merge_groups(outs, maxes, denoms, q.dtype), state


def short_conv(u, prev, conv_w):
    t = u.shape[1]
    uc = jnp.concatenate([prev.astype(u.dtype), u], axis=1)
    y = sum(conv_w[j] * uc[:, j:j + t] for j in range(CONV_WIDTH))
    return y, uc[:, -(CONV_WIDTH - 1):]


def layer(h, ple, pos, attn_fn, conv_prev, w_in, conv_w, w_attn_out, w_conv_out, w_o,
          g_pre_mix, g_post_mix, w_up, w_down, g_pre_mlp, g_post_mlp, g_ple, w_ple_gate, w_ple_proj):
    bt, t, _ = h.shape
    xn = rmsnorm(h, g_pre_mix)
    proj = xn @ w_in
    q, k, v, cb, cc, ch, ga, gb = jnp.split(proj, SPLITS, axis=-1)
    nq = N_GROUPS * HEADS_PER_GROUP
    q = rope(q.reshape(bt, t, nq, HEAD_DIM), pos)
    k = rope(k.reshape(bt, t, nq, HEAD_DIM), pos)
    v = v.reshape(bt, t, nq, HEAD_DIM)
    attn, attn_state = attn_fn(q, k, v)
    conv, conv_state = short_conv(cc * ch, conv_prev, conv_w)
    a = attn @ w_attn_out
    c = (cb * conv) @ w_conv_out
    mix = (jax.nn.sigmoid(ga) * a + jax.nn.sigmoid(gb) * c) @ w_o
    h = h + rmsnorm(mix, g_post_mix)
    f = rmsnorm(h, g_pre_mlp) @ w_up
    f = jnp.square(jax.nn.relu(f)) @ w_down
    h = h + rmsnorm(f, g_post_mlp)
    h = h + jax.nn.sigmoid(rmsnorm(h, g_ple) @ w_ple_gate) * (ple @ w_ple_proj)
    return h, attn_state, conv_state


def setup_inputs(seed: int = 0) -> dict:
    key = jax.random.key(seed)
    ks = iter(jax.random.split(key, 40))

    def nrm(shape, scale=1.0):
        return scale * jax.random.normal(next(ks), shape, jnp.float32)

    def gain():
        return 1.0 + nrm((DEPTH, D_MODEL), 0.05)

    inp = {}
    inp['x_prompt'] = nrm((BATCH, SEQ, D_MODEL))
    inp['x_sample'] = nrm((DEC_BATCH, DEC_SEQ, D_MODEL))
    inp['p_prompt'] = nrm((DEPTH, BATCH, SEQ, PLE_DIM))
    inp['p_sample'] = nrm((DEPTH, DEC_BATCH, DEC_SEQ, PLE_DIM))
    for window, _ in GROUPS:
        lb = min(window, PAST_LEN)
        inp['cache_k_w%d' % window] = nrm((DEPTH, DEC_BATCH, lb, HEADS_PER_GROUP, HEAD_DIM))
        inp['cache_v_w%d' % window] = nrm((DEPTH, DEC_BATCH, lb, HEADS_PER_GROUP, HEAD_DIM))
    inp['state_conv'] = nrm((DEPTH, DEC_BATCH, CONV_WIDTH - 1, CONV_CHANNELS))
    inp['w_in'] = nrm((DEPTH, D_MODEL, IN_PROJ_WIDTH), D_MODEL ** -0.5)
    inp['conv_w'] = nrm((DEPTH, CONV_WIDTH, CONV_CHANNELS), CONV_WIDTH ** -0.5)
    inp['w_attn_out'] = nrm((DEPTH, ATTN_OUT_WIDTH, D_MODEL), ATTN_OUT_WIDTH ** -0.5)
    inp['w_conv_out'] = nrm((DEPTH, CONV_CHANNELS, D_MODEL), CONV_CHANNELS ** -0.5)
    inp['w_o'] = nrm((DEPTH, D_MODEL, D_MODEL), D_MODEL ** -0.5)
    inp['g_pre_mix'] = gain()
    inp['g_post_mix'] = gain()
    inp['w_up'] = nrm((DEPTH, D_MODEL, D_FF), D_MODEL ** -0.5)
    inp['w_down'] = nrm((DEPTH, D_FF, D_MODEL), D_FF ** -0.5)
    inp['g_pre_mlp'] = gain()
    inp['g_post_mlp'] = gain()
    inp['g_ple'] = gain()
    inp['w_ple_gate'] = nrm((DEPTH, D_MODEL, D_MODEL), D_MODEL ** -0.5)
    inp['w_ple_proj'] = nrm((DEPTH, PLE_DIM, D_MODEL), PLE_DIM ** -0.5)
    return inp


def reference(x_prompt, x_sample, p_prompt, p_sample,
              cache_k_w128, cache_v_w128, cache_k_w512, cache_v_w512, cache_k_w2048, cache_v_w2048,
              state_conv, w_in, conv_w, w_attn_out, w_conv_out, w_o, g_pre_mix, g_post_mix,
              w_up, w_down, g_pre_mlp, g_post_mlp, g_ple, w_ple_gate, w_ple_proj):
    pos_p = jnp.arange(x_prompt.shape[1], dtype=jnp.int32)
    pos_s = PAST_LEN + jnp.arange(x_sample.shape[1], dtype=jnp.int32)
    conv_zero = jnp.zeros((x_prompt.shape[0], CONV_WIDTH - 1, CONV_CHANNELS), x_prompt.dtype)
    hp, hs = x_prompt, x_sample
    st_p, st_s = [], []
    for i in range(DEPTH):
        lw = (w_in[i], conv_w[i], w_attn_out[i], w_conv_out[i], w_o[i], g_pre_mix[i], g_post_mix[i],
              w_up[i], w_down[i], g_pre_mlp[i], g_post_mlp[i], g_ple[i], w_ple_gate[i], w_ple_proj[i])
        hp, a_p, c_p = layer(hp, p_prompt[i], pos_p, attn_prompt, conv_zero, *lw)
        bufs = (cache_k_w128[i], cache_v_w128[i], cache_k_w512[i], cache_v_w512[i],
                cache_k_w2048[i], cache_v_w2048[i])
        hs, a_s, c_s = layer(hs, p_sample[i], pos_s, functools.partial(attn_sample, bufs=bufs),
                             state_conv[i], *lw)
        st_p.append(a_p + [c_p])
        st_s.append(a_s + [c_s])
    (k128_p, v128_p, k512_p, v512_p, k2048_p, v2048_p, conv_p) = [jnp.stack(col) for col in zip(*st_p)]
    (k128_s, v128_s, k512_s, v512_s, k2048_s, v2048_s, conv_s) = [jnp.stack(col) for col in zip(*st_s)]
    return (hp, hs, k128_p, v128_p, k512_p, v512_p, k2048_p, v2048_p, conv_p,
            k128_s, v128_s, k512_s, v512_s, k2048_s, v2048_s, conv_s)
```

```python
import functools

import jax
import jax.numpy as jnp
from jax import lax
from jax.experimental import pallas as pl
from jax.experimental.pallas import tpu as pltpu

F32 = jnp.float32
BF16 = jnp.bfloat16

D_MODEL = 1024
DEPTH = 4
PAST_LEN = 8192
HEAD_DIM = 64
HEADS_PER_GROUP = 4
GROUPS = ((128, 1), (512, 4), (2048, 16))
N_GROUPS = len(GROUPS)
GROUP_WIDTH = HEADS_PER_GROUP * HEAD_DIM
ATTN_WIDTH = N_GROUPS * GROUP_WIDTH
CONV_CHANNELS = 768
CONV_WIDTH = 3
D_FF = 4 * D_MODEL
PLE_DIM = 256
ROPE_THETA = 10000.0
BLOCK = 128
RMS_EPS = 1e-6
NEG_INF = -1e30
IN_PROJ_WIDTH = 3 * ATTN_WIDTH + 3 * CONV_CHANNELS + 2 * D_MODEL
OFF_Q, OFF_K, OFF_V = 0, ATTN_WIDTH, 2 * ATTN_WIDTH
OFF_CB = 3 * ATTN_WIDTH
OFF_CC = OFF_CB + CONV_CHANNELS
OFF_CH = OFF_CC + CONV_CHANNELS
OFF_GA = OFF_CH + CONV_CHANNELS
OFF_GB = OFF_GA + D_MODEL

V7X_LANES = 128
V7X_SUBLANES = 8
V7X_VMEM_LIMIT = 56 * 1024 * 1024
SLABS = ATTN_WIDTH // V7X_LANES
SLABS_PER_GROUP = GROUP_WIDTH // V7X_LANES
ROW_TILE = 512
SPAN = BLOCK * GROUPS[-1][1]
MLP_CHUNK = 1024


def _rms(x, g):
    return x * lax.rsqrt(jnp.mean(x * x, axis=-1, keepdims=True) + RMS_EPS) * g


def _dot(a, b):
    return jnp.dot(a, b, preferred_element_type=F32)


def _resident(shape, index):
    return pl.BlockSpec(shape, index, pipeline_mode=pl.Buffered(1))


def _inproj_kernel(sample, tm, tiles_per_seq, *refs):
    if sample:
        (x_ref, cs_ref, sn_ref, g_ref, w_ref, cw_ref, p0_ref, p1_ref,
         q_ref, k_ref, v_ref, cbc_ref, sg_ref, u_ref) = refs
    else:
        (x_ref, cs_ref, sn_ref, g_ref, w_ref, cw_ref,
         q_ref, k_ref, v_ref, cbc_ref, sg_ref, cst_ref, u_scr) = refs

    xn = _rms(x_ref[...], g_ref[...]).astype(BF16)

    def proj(lo, width):
        return _dot(xn, w_ref[:, lo:lo + width])

    cs = cs_ref[...]
    sn = sn_ref[...]
    lane = lax.broadcasted_iota(jnp.int32, (tm, V7X_LANES), 1)
    lower = (lane % HEAD_DIM) < (HEAD_DIM // 2)

    def rope_store(t, out_ref, scale):
        for c in range(SLABS):
            xc = t[:, c * V7X_LANES:(c + 1) * V7X_LANES]
            sw = jnp.where(lower, pltpu.roll(xc, V7X_LANES - HEAD_DIM // 2, 1),
                           pltpu.roll(xc, HEAD_DIM // 2, 1))
            r = xc * cs + sw * sn
            if scale is not None:
                r = r * scale
            out_ref[c] = r

    rope_store(proj(OFF_Q, ATTN_WIDTH), q_ref, HEAD_DIM ** -0.5)
    rope_store(proj(OFF_K, ATTN_WIDTH), k_ref, None)
    vv = proj(OFF_V, ATTN_WIDTH)
    for c in range(SLABS):
        v_ref[c] = vv[:, c * V7X_LANES:(c + 1) * V7X_LANES]

    cb = proj(OFF_CB, CONV_CHANNELS)
    u = proj(OFF_CC, CONV_CHANNELS) * proj(OFF_CH, CONV_CHANNELS)
    cw = cw_ref[...]
    if sample:
        conv = cw[0:1] * p0_ref[...] + cw[1:2] * p1_ref[...] + cw[2:3] * u
        u_ref[...] = u
    else:
        halo = V7X_SUBLANES

        @pl.when(pl.program_id(0) % tiles_per_seq == 0)
        def _():
            u_scr[0:halo, :] = jnp.zeros((halo, CONV_CHANNELS), F32)

        u_scr[halo:tm + halo, :] = u
        conv = (cw[0:1] * u_scr[halo - 2:tm + halo - 2, :]
                + cw[1:2] * u_scr[halo - 1:tm + halo - 1, :] + cw[2:3] * u)
        tail = u_scr[tm:tm + halo, :]
        u_scr[0:halo, :] = tail
        cst_ref[...] = tail
    cbc_ref[...] = (cb * conv).astype(BF16)
    sg_ref[:, 0:D_MODEL] = jax.nn.sigmoid(proj(OFF_GA, D_MODEL)).astype(BF16)
    sg_ref[:, D_MODEL:2 * D_MODEL] = jax.nn.sigmoid(proj(OFF_GB, D_MODEL)).astype(BF16)


def _inproj(layer, x, cs, sn, gains, w_in, conv_w, *, seq_len=None, prev=None):
    rows = x.shape[0]
    sample = prev is not None
    tm = rows if sample else ROW_TILE
    tiles_per_seq = 1 if sample else seq_len // tm
    grid = (rows // tm,)
    row = lambda i: (i, 0)
    slab_spec = pl.BlockSpec((SLABS, tm, V7X_LANES), lambda i: (0, i, 0))
    table_spec = pl.BlockSpec((tm, V7X_LANES), lambda i: (i % tiles_per_seq, 0))
    in_specs = [
        pl.BlockSpec((tm, D_MODEL), row),
        table_spec, table_spec,
        pl.BlockSpec((None, 1, D_MODEL), lambda i: (layer, 0, 0)),
        _resident((None, D_MODEL, IN_PROJ_WIDTH), lambda i: (layer, 0, 0)),
        pl.BlockSpec((None, CONV_WIDTH, CONV_CHANNELS), lambda i: (layer, 0, 0)),
    ]
    args = [x, cs, sn, gains, w_in, conv_w]
    slab_shape = jax.ShapeDtypeStruct((SLABS, rows, V7X_LANES), F32)
    out_shape = [slab_shape, slab_shape, slab_shape,
                 jax.ShapeDtypeStruct((rows, CONV_CHANNELS), BF16),
                 jax.ShapeDtypeStruct((rows, 2 * D_MODEL), BF16)]
    out_specs = [slab_spec, slab_spec, slab_spec,
                 pl.BlockSpec((tm, CONV_CHANNELS), row),
                 pl.BlockSpec((tm, 2 * D_MODEL), row)]
    scratch = []
    if sample:
        in_specs += [pl.BlockSpec((tm, CONV_CHANNELS), row)] * 2
        args += list(prev)
        out_shape.append(jax.ShapeDtypeStruct((rows, CONV_CHANNELS), F32))
        out_specs.append(pl.BlockSpec((tm, CONV_CHANNELS), row))
    else:
        n_seq = rows // seq_len
        out_shape.append(jax.ShapeDtypeStruct((n_seq, V7X_SUBLANES, CONV_CHANNELS), F32))
        out_specs.append(pl.BlockSpec((None, V7X_SUBLANES, CONV_CHANNELS),
                                      lambda i: (i // tiles_per_seq, 0, 0)))
        scratch.append(pltpu.VMEM((tm + V7X_SUBLANES, CONV_CHANNELS), F32))
    return pl.pallas_call(
        functools.partial(_inproj_kernel, sample, tm, tiles_per_seq),
        out_shape=out_shape, grid=grid, in_specs=in_specs, out_specs=out_specs,
        scratch_shapes=scratch,
        name="inproj_sample" if sample else "inproj_prompt",
        compiler_params=pltpu.CompilerParams(
            dimension_semantics=("arbitrary",), vmem_limit_bytes=V7X_VMEM_LIMIT),
    )(*args)


def _attn_kernel(dil, q_ref, k_ref, kp_ref, v_ref, vp_ref, o_ref, m_ref, l_ref, kbuf, vbuf):
    prev_rows = BLOCK * dil
    first_span = pl.program_id(1) == 0
    kbuf[:, 0:prev_rows, :] = kp_ref[...]
    kbuf[:, prev_rows:prev_rows + SPAN, :] = k_ref[...]
    vbuf[:, 0:prev_rows, :] = vp_ref[...]
    vbuf[:, prev_rows:prev_rows + SPAN, :] = v_ref[...]

    qi = lax.broadcasted_iota(jnp.int32, (2 * BLOCK, 2 * BLOCK), 0) % BLOCK
    kj = lax.broadcasted_iota(jnp.int32, (2 * BLOCK, 2 * BLOCK), 1)
    own = kj >= BLOCK
    slack = jnp.where(own, qi - (kj - BLOCK), kj - qi)
    prev_col = jnp.where(own, 0, 2 * BLOCK)
    low_q = lax.broadcasted_iota(jnp.int32, (BLOCK, V7X_LANES), 1) < HEAD_DIM
    low_kv = lax.broadcasted_iota(jnp.int32, (2 * BLOCK, V7X_LANES), 1) < HEAD_DIM

    def rows_at(start):
        return pl.ds(start, BLOCK, stride=dil) if dil > 1 else pl.ds(start, BLOCK)

    def body(blk, carry):
        res = blk % dil
        sub = blk // dil
        row0 = res + prev_rows * sub
        no_prev = jnp.logical_and(first_span, sub == 0).astype(jnp.int32)
        valid = slack - prev_col * no_prev >= 0
        for c in range(SLABS_PER_GROUP):
            q = q_ref[c, rows_at(row0), :]
            k2 = jnp.concatenate([kbuf[c, rows_at(row0), :],
                                  kbuf[c, rows_at(row0 + prev_rows), :]], axis=0)
            v2 = jnp.concatenate([vbuf[c, rows_at(row0), :],
                                  vbuf[c, rows_at(row0 + prev_rows), :]], axis=0)
            q2 = jnp.concatenate([jnp.where(low_q, q, 0.0), jnp.where(low_q, 0.0, q)], axis=0)
            s = lax.dot_general(q2.astype(BF16), k2.astype(BF16), (((1,), (1,)), ((), ())),
                                preferred_element_type=F32)
            s = jnp.where(valid, s, NEG_INF)
            m = jnp.max(s, axis=-1, keepdims=True)
            p = jnp.exp(s - m)
            l = jnp.sum(p, axis=-1, keepdims=True)
            pb = p.astype(BF16)
            p_cat = jnp.concatenate([pb[0:BLOCK], pb[BLOCK:2 * BLOCK]], axis=1)
            v_stack = jnp.concatenate([jnp.where(low_kv, v2, 0.0), jnp.where(low_kv, 0.0, v2)],
                                      axis=0).astype(BF16)
            o = _dot(p_cat, v_stack)
            l_slab = jnp.where(low_q, l[0:BLOCK], l[BLOCK:2 * BLOCK])
            m_slab = jnp.where(low_q, m[0:BLOCK], m[BLOCK:2 * BLOCK])
            o_ref[c, rows_at(row0), :] = o / l_slab
            m_ref[c, rows_at(row0), :] = m_slab
            l_ref[c, rows_at(row0), :] = l_slab
        return carry

    lax.fori_loop(0, SPAN // BLOCK, body, 0)


def _attn_group(group, q, k, v, n_seq, seq_len):
    dil = GROUPS[group][1]
    rows = q.shape[1]
    prev_rows = BLOCK * dil
    spans_per_seq = seq_len // SPAN
    blocks_per_span = SPAN // prev_rows

    def cur(b, n):
        return (group, b * spans_per_seq + n, 0)

    def prev(b, n):
        return (group, jnp.maximum((b * spans_per_seq + n) * blocks_per_span - 1, 0), 0)

    cur_spec = pl.BlockSpec((SLABS_PER_GROUP, SPAN, V7X_LANES), cur)
    prev_spec = pl.BlockSpec((SLABS_PER_GROUP, prev_rows, V7X_LANES), prev)
    out_spec = pl.BlockSpec((SLABS_PER_GROUP, SPAN, V7X_LANES),
                            lambda b, n: (0, b * spans_per_seq + n, 0))
    out = jax.ShapeDtypeStruct((SLABS_PER_GROUP, rows, V7X_LANES), F32)
    buf = pltpu.VMEM((SLABS_PER_GROUP, prev_rows + SPAN, V7X_LANES), F32)
    return pl.pallas_call(
        functools.partial(_attn_kernel, dil),
        out_shape=[out, out, out], grid=(n_seq, spans_per_seq),
        in_specs=[cur_spec, cur_spec, prev_spec, cur_spec, prev_spec],
        out_specs=[out_spec, out_spec, out_spec],
        scratch_shapes=[buf, buf],
        name="attn_prompt_d%d" % dil,
        compiler_params=pltpu.CompilerParams(
            dimension_semantics=("arbitrary", "arbitrary"), vmem_limit_bytes=V7X_VMEM_LIMIT),
    )(q, k, k, v, v)


def _tail_kernel(merge, *refs):
    if merge:
        h_ref, o0, m0, l0, o1, m1, l1, o2, m2, l2 = refs[:10]
        rest = refs[10:]
    else:
        h_ref, attn_ref = refs[:2]
        rest = refs[2:]
    (cbc_ref, sg_ref, ple_ref, wa_ref, wc_ref, wo_ref, wu_ref, wd_ref, wg_ref, wp_ref,
     gains_ref, out_ref) = rest

    if merge:
        ms = [m0[...], m1[...], m2[...]]
        mx = jnp.maximum(jnp.maximum(ms[0], ms[1]), ms[2])
        ws = [l[...] * jnp.exp(m - mx) for m, l in zip(ms, (l0, l1, l2))]
        num = ws[0] * o0[...] + ws[1] * o1[...] + ws[2] * o2[...]
        attn3 = num / (ws[0] + ws[1] + ws[2])
        attn = jnp.concatenate([attn3[c] for c in range(SLABS_PER_GROUP)], axis=1)
    else:
        attn = attn_ref[...]

    h = h_ref[...]
    a = _dot(attn.astype(BF16), wa_ref[...])
    c = _dot(cbc_ref[...], wc_ref[...])
    sg = sg_ref[...].astype(F32)
    mix = sg[:, 0:D_MODEL] * a + sg[:, D_MODEL:2 * D_MODEL] * c
    h = h + _rms(_dot(mix.astype(BF16), wo_ref[...]), gains_ref[1:2, :])

    xn = _rms(h, gains_ref[2:3, :]).astype(BF16)
    acc = None
    for lo in range(0, D_FF, MLP_CHUNK):
        f = _dot(xn, wu_ref[:, lo:lo + MLP_CHUNK])
        f = jnp.square(jnp.maximum(f, 0.0)).astype(BF16)
        t = _dot(f, wd_ref[lo:lo + MLP_CHUNK, :])
        acc = t if acc is None else acc + t
    h = h + _rms(acc, gains_ref[3:4, :])

    gate = jax.nn.sigmoid(_dot(_rms(h, gains_ref[4:5, :]).astype(BF16), wg_ref[...]))
    out_ref[...] = h + gate * _dot(ple_ref[...].astype(BF16), wp_ref[...])


def _tail(layer, h, attn_parts, cbc, sg, ple, weights, gains):
    rows = h.shape[0]
    merge = len(attn_parts) > 1
    tm = ROW_TILE if merge else rows
    row = lambda i: (i, 0)
    in_specs = [pl.BlockSpec((tm, D_MODEL), row)]
    if merge:
        in_specs += [pl.BlockSpec((SLABS_PER_GROUP, tm, V7X_LANES), lambda i: (0, i, 0))] * 9
    else:
        in_specs += [pl.BlockSpec((tm, GROUP_WIDTH), row)]
    in_specs += [
        pl.BlockSpec((tm, CONV_CHANNELS), row),
        pl.BlockSpec((tm, 2 * D_MODEL), row),
        pl.BlockSpec((None, tm, PLE_DIM), lambda i: (layer, i, 0)),
    ]
    in_specs += [_resident((None,) + w.shape[1:], lambda i: (layer, 0, 0)) for w in weights]
    in_specs += [pl.BlockSpec((None,) + gains.shape[1:], lambda i: (layer, 0, 0))]
    return pl.pallas_call(
        functools.partial(_tail_kernel, merge),
        out_shape=jax.ShapeDtypeStruct((rows, D_MODEL), F32),
        grid=(rows // tm,), in_specs=in_specs,
        out_specs=pl.BlockSpec((tm, D_MODEL), row),
        name="tail_prompt" if merge else "tail_sample",
        compiler_params=pltpu.CompilerParams(
            dimension_semantics=("arbitrary",), vmem_limit_bytes=V7X_VMEM_LIMIT),
    )(h, *attn_parts, cbc, sg, ple, *weights, gains)


def _shift_in(old, new_col):
    width = old.shape[1]
    tiles = width // V7X_LANES
    lane = lax.broadcasted_iota(jnp.int32, (old.shape[0], V7X_LANES), 1)
    last = lane == V7X_LANES - 1
    rot = [pltpu.roll(old[:, t * V7X_LANES:(t + 1) * V7X_LANES], V7X_LANES - 1, 1)
           for t in range(tiles)]
    out = [jnp.where(last, rot[t + 1] if t + 1 < tiles else new_col, rot[t]) for t in range(tiles)]
    return out[0] if tiles == 1 else jnp.concatenate(out, axis=1)


def _sample_attn_kernel(n_alias, *refs):
    q_ref, kn_ref, vn_ref = refs[:3]
    caches = refs[3:3 + 2 * N_GROUPS]
    attn_ref = refs[3 + 2 * N_GROUPS + n_alias]
    outs = refs[4 + 2 * N_GROUPS + n_alias:]

    q = q_ref[...]
    kn = kn_ref[...]
    vn = vn_ref[...]
    head_row = lax.broadcasted_iota(jnp.int32, (V7X_SUBLANES, GROUP_WIDTH), 0)
    head_lane = lax.broadcasted_iota(jnp.int32, (V7X_SUBLANES, GROUP_WIDTH), 1) // HEAD_DIM
    own_head = head_row == head_lane

    parts = []
    for g, (window, dil) in enumerate(GROUPS):
        sl = slice(g * GROUP_WIDTH, (g + 1) * GROUP_WIDTH)
        kt = caches[2 * g][...].reshape(GROUP_WIDTH, window)
        vt = caches[2 * g + 1][...].reshape(GROUP_WIDTH, window)
        q_heads = jnp.where(own_head, jnp.broadcast_to(q[:, sl], own_head.shape), 0.0)
        s = _dot(q_heads.astype(BF16), kt.astype(BF16))
        pos = lax.broadcasted_iota(jnp.int32, s.shape, 1)
        s = jnp.where(pos % dil == 0, s, NEG_INF)
        s_new = jnp.sum(q_heads * kn[:, sl], axis=-1, keepdims=True)
        m = jnp.maximum(jnp.max(s, axis=-1, keepdims=True), s_new)
        p = jnp.exp(s - m)
        p_new = jnp.exp(s_new - m)
        l = jnp.sum(p, axis=-1, keepdims=True) + p_new
        o = lax.dot_general(p.astype(BF16), vt.astype(BF16), (((1,), (1,)), ((), ())),
                            preferred_element_type=F32) + p_new * vn[:, sl]
        parts.append((o / l, m, l))

        k_col = jnp.transpose(jnp.broadcast_to(kn[:, sl], (V7X_LANES, GROUP_WIDTH)))
        v_col = jnp.transpose(jnp.broadcast_to(vn[:, sl], (V7X_LANES, GROUP_WIDTH)))
        outs[2 * g][...] = _shift_in(kt, k_col).reshape(HEADS_PER_GROUP, HEAD_DIM, window)
        outs[2 * g + 1][...] = _shift_in(vt, v_col).reshape(HEADS_PER_GROUP, HEAD_DIM, window)

    mx = jnp.maximum(jnp.maximum(parts[0][1], parts[1][1]), parts[2][1])
    ws = [l * jnp.exp(m - mx) for _, m, l in parts]
    num = ws[0] * parts[0][0] + ws[1] * parts[1][0] + ws[2] * parts[2][0]
    merged = num / (ws[0] + ws[1] + ws[2])
    attn_ref[...] = jnp.sum(jnp.where(own_head, merged, 0.0), axis=0, keepdims=True)


def _sample_attn(layer, q, k_new, v_new, caches_t, new_caches):
    batch = q.shape[0]
    vec_spec = pl.BlockSpec((None, 1, ATTN_WIDTH), lambda b: (b, 0, 0))
    in_specs = [vec_spec, vec_spec, vec_spec]
    cache_specs = [pl.BlockSpec((None, None) + c.shape[2:], lambda b: (layer, b, 0, 0, 0))
                   for c in caches_t]
    in_specs += cache_specs
    args = [q, k_new, v_new, *caches_t]
    aliases = {}
    n_alias = 0
    if new_caches is not None:
        n_alias = len(new_caches)
        in_specs += [pl.BlockSpec(memory_space=pl.ANY)] * n_alias
        aliases = {len(args) + j: 1 + j for j in range(n_alias)}
        args += list(new_caches)
    out_shape = [jax.ShapeDtypeStruct((batch, 1, GROUP_WIDTH), F32)]
    out_shape += [jax.ShapeDtypeStruct(c.shape, F32) for c in caches_t]
    out_specs = [pl.BlockSpec((None, 1, GROUP_WIDTH), lambda b: (b, 0, 0))] + cache_specs
    res = pl.pallas_call(
        functools.partial(_sample_attn_kernel, n_alias),
        out_shape=out_shape, grid=(batch,), in_specs=in_specs, out_specs=out_specs,
        input_output_aliases=aliases, name="attn_sample",
        compiler_params=pltpu.CompilerParams(
            dimension_semantics=("arbitrary",), vmem_limit_bytes=V7X_VMEM_LIMIT),
    )(*args)
    return res[0], res[1:]


def _rope_tables(pos):
    half = HEAD_DIM // 2
    inv = jnp.power(ROPE_THETA, -2.0 * jnp.arange(half, dtype=F32) / HEAD_DIM)
    ang = pos.astype(F32)[:, None] * inv[None, :]
    cos, sin = jnp.cos(ang), jnp.sin(ang)
    reps = V7X_LANES // HEAD_DIM
    return (jnp.concatenate([cos, cos] * reps, axis=1), jnp.concatenate([-sin, sin] * reps, axis=1))


def kernel(x_prompt, x_sample, p_prompt, p_sample, cache_k_w128, cache_v_w128, cache_k_w512, cache_v_w512, cache_k_w2048, cache_v_w2048, state_conv, w_in, conv_w, w_attn_out, w_conv_out, w_o, g_pre_mix, g_post_mix, w_up, w_down, g_pre_mlp, g_post_mlp, g_ple, w_ple_gate, w_ple_proj):
    n_seq, seq_len, _ = x_prompt.shape
    n_dec = x_sample.shape[0]
    rows = n_seq * seq_len

    w_in_b = w_in.astype(BF16)
    tail_w = [w.astype(BF16) for w in (w_attn_out, w_conv_out, w_o, w_up, w_down, w_ple_gate, w_ple_proj)]
    gains = jnp.stack([g_pre_mix, g_post_mix, g_pre_mlp, g_post_mlp, g_ple], axis=1)
    g_in = g_pre_mix[:, None, :]

    cs_p, sn_p = _rope_tables(jnp.arange(seq_len, dtype=jnp.int32))
    cs_s, sn_s = _rope_tables(jnp.full((n_dec,), PAST_LEN, dtype=jnp.int32))

    caches = (cache_k_w128, cache_v_w128, cache_k_w512, cache_v_w512, cache_k_w2048, cache_v_w2048)
    caches_t = [jnp.transpose(c, (0, 1, 3, 4, 2)) for c in caches]

    hp = x_prompt.reshape(rows, D_MODEL)
    hs = x_sample.reshape(n_dec, D_MODEL)
    ple_p = p_prompt.reshape(DEPTH, rows, PLE_DIM)
    ple_s = p_sample.reshape(DEPTH, n_dec, PLE_DIM)

    state_p = [[] for _ in range(2 * N_GROUPS)]
    conv_p, conv_s = [], []
    new_caches = None
    for i in range(DEPTH):
        q, k, v, cbc, sg, cst = _inproj(i, hp, cs_p, sn_p, g_in, w_in_b, conv_w, seq_len=seq_len)
        parts = []
        for g in range(N_GROUPS):
            parts += _attn_group(g, q, k, v, n_seq, seq_len)
        hp = _tail(i, hp, parts, cbc, sg, ple_p, tail_w, gains)
        conv_p.append(cst[:, V7X_SUBLANES - (CONV_WIDTH - 1):, :])
        for g, (window, _) in enumerate(GROUPS):
            keep = min(window, seq_len)
            for j, t in enumerate((k, v)):
                st = t[g * SLABS_PER_GROUP:(g + 1) * SLABS_PER_GROUP]
                st = st.reshape(SLABS_PER_GROUP, n_seq, seq_len, V7X_LANES)[:, :, seq_len - keep:, :]
                st = jnp.transpose(st, (1, 2, 0, 3)).reshape(n_seq, keep, HEADS_PER_GROUP, HEAD_DIM)
                state_p[2 * g + j].append(st)

        prev = (state_conv[i, :, 0, :], state_conv[i, :, 1, :])
        qs, ks, vs, cbcs, sgs, us = _inproj(i, hs, cs_s, sn_s, g_in, w_in_b, conv_w, prev=prev)
        as_rows = lambda t: jnp.transpose(t, (1, 0, 2)).reshape(n_dec, 1, ATTN_WIDTH)
        attn_s, new_caches = _sample_attn(i, as_rows(qs), as_rows(ks), as_rows(vs), caches_t, new_caches)
        hs = _tail(i, hs, [attn_s.reshape(n_dec, GROUP_WIDTH)], cbcs, sgs, ple_s, tail_w, gains)
        conv_s.append(jnp.stack([prev[1], us], axis=1))

    out_p = [jnp.stack(col) for col in state_p]
    out_s = [jnp.transpose(c, (0, 1, 4, 2, 3)) for c in new_caches]
    return (hp.reshape(n_seq, seq_len, D_MODEL), hs.reshape(n_dec, 1, D_MODEL),
            *out_p, jnp.stack(conv_p), *out_s, jnp.stack(conv_s))
```

```python
import functools

import jax
import jax.numpy as jnp
from jax import lax
from jax.experimental import pallas as pl
from jax.experimental.pallas import tpu as pltpu

F32 = jnp.float32
BF16 = jnp.bfloat16

D_MODEL = 1024
DEPTH = 4
PAST_LEN = 8192
HEAD_DIM = 64
HEADS_PER_GROUP = 4
GROUPS = ((128, 1), (512, 4), (2048, 16))
N_GROUPS = len(GROUPS)
GROUP_WIDTH = HEADS_PER_GROUP * HEAD_DIM
ATTN_WIDTH = N_GROUPS * GROUP_WIDTH
CONV_CHANNELS = 768
CONV_WIDTH = 3
D_FF = 4 * D_MODEL
PLE_DIM = 256
ROPE_THETA = 10000.0
BLOCK = 128
RMS_EPS = 1e-6
NEG_INF = -1e30
IN_PROJ_WIDTH = 3 * ATTN_WIDTH + 3 * CONV_CHANNELS + 2 * D_MODEL
OFF_Q, OFF_K, OFF_V = 0, ATTN_WIDTH, 2 * ATTN_WIDTH
OFF_CB = 3 * ATTN_WIDTH
OFF_CC = OFF_CB + CONV_CHANNELS
OFF_CH = OFF_CC + CONV_CHANNELS
OFF_GA = OFF_CH + CONV_CHANNELS
OFF_GB = OFF_GA + D_MODEL

V7X_LANES = 128
V7X_SUBLANES = 8
V7X_VMEM_LIMIT = 56 * 1024 * 1024
SLABS = ATTN_WIDTH // V7X_LANES
SLABS_PER_GROUP = GROUP_WIDTH // V7X_LANES
HEADS_PER_SLAB = V7X_LANES // HEAD_DIM
ROW_TILE = 512
SPAN = BLOCK * GROUPS[-1][1]
MLP_CHUNK = 1024
STRIDE_STEP = 4
ATTN_UNROLL = 4


def _rms(x, g):
    return x * lax.rsqrt(jnp.mean(x * x, axis=-1, keepdims=True) + RMS_EPS) * g


def _dot(a, b):
    return jnp.dot(a, b, preferred_element_type=F32)


def _resident(shape, index):
    return pl.BlockSpec(shape, index, pipeline_mode=pl.Buffered(1))


def _inproj_kernel(sample, tm, tiles_per_seq, n_alias, *refs):
    if sample:
        (x_ref, cs_ref, sn_ref, g_ref, w_ref, cw_ref, p0_ref, p1_ref,
         q_ref, k_ref, v_ref, cbc_ref, sg_ref, u_ref) = refs
    else:
        x_ref, cs_ref, sn_ref, g_ref, w_ref, cw_ref = refs[:6]
        q_ref, k_ref, v_ref, cbc_ref, sg_ref, cst_ref = refs[6 + n_alias:12 + n_alias]
        state_refs = refs[12 + n_alias:12 + n_alias + 2 * N_GROUPS]
        u_scr = refs[-1]

    xn = _rms(x_ref[...], g_ref[...]).astype(BF16)

    def proj(lo, width):
        return _dot(xn, w_ref[:, lo:lo + width])

    cs = cs_ref[...]
    sn = sn_ref[...]
    lane = lax.broadcasted_iota(jnp.int32, (tm, V7X_LANES), 1)
    lower = (lane % HEAD_DIM) < (HEAD_DIM // 2)

    def rope_store(t, out_ref, scale):
        for c in range(SLABS):
            xc = t[:, c * V7X_LANES:(c + 1) * V7X_LANES]
            sw = jnp.where(lower, pltpu.roll(xc, V7X_LANES - HEAD_DIM // 2, 1),
                           pltpu.roll(xc, HEAD_DIM // 2, 1))
            r = xc * cs + sw * sn
            if scale is not None:
                r = r * scale
            out_ref[c] = r

    rope_store(proj(OFF_Q, ATTN_WIDTH), q_ref, HEAD_DIM ** -0.5)
    rope_store(proj(OFF_K, ATTN_WIDTH), k_ref, None)
    vv = proj(OFF_V, ATTN_WIDTH)
    for c in range(SLABS):
        v_ref[c] = vv[:, c * V7X_LANES:(c + 1) * V7X_LANES]

    if not sample:
        tile = pl.program_id(0) % tiles_per_seq
        for g, (window, _) in enumerate(GROUPS):
            keep = min(window, tm)
            first_tile = tiles_per_seq - max(window // tm, 1)

            @pl.when(tile >= first_tile)
            def _(g=g, keep=keep):
                for j, src in enumerate((k_ref, v_ref)):
                    for half in range(SLABS_PER_GROUP):
                        rows = src[g * SLABS_PER_GROUP + half, tm - keep:tm, :]
                        heads = slice(half * HEADS_PER_SLAB, (half + 1) * HEADS_PER_SLAB)
                        state_refs[2 * g + j][heads, :, :] = jnp.transpose(rows).reshape(
                            HEADS_PER_SLAB, HEAD_DIM, keep)

    cb = proj(OFF_CB, CONV_CHANNELS)
    u = proj(OFF_CC, CONV_CHANNELS) * proj(OFF_CH, CONV_CHANNELS)
    cw = cw_ref[...]
    if sample:
        conv = cw[0:1] * p0_ref[...] + cw[1:2] * p1_ref[...] + cw[2:3] * u
        u_ref[...] = u
    else:
        halo = V7X_SUBLANES

        @pl.when(pl.program_id(0) % tiles_per_seq == 0)
        def _():
            u_scr[0:halo, :] = jnp.zeros((halo, CONV_CHANNELS), F32)

        u_scr[halo:tm + halo, :] = u
        conv = (cw[0:1] * u_scr[halo - 2:tm + halo - 2, :]
                + cw[1:2] * u_scr[halo - 1:tm + halo - 1, :] + cw[2:3] * u)
        tail = u_scr[tm:tm + halo, :]
        u_scr[0:halo, :] = tail
        cst_ref[...] = tail
    cbc_ref[...] = (cb * conv).astype(BF16)
    sg_ref[:, 0:D_MODEL] = jax.nn.sigmoid(proj(OFF_GA, D_MODEL)).astype(BF16)
    sg_ref[:, D_MODEL:2 * D_MODEL] = jax.nn.sigmoid(proj(OFF_GB, D_MODEL)).astype(BF16)


def _inproj(layer, x, cs, sn, gains, w_in, conv_w, *, seq_len=None, prev=None, states=None):
    rows = x.shape[0]
    sample = prev is not None
    tm = rows if sample else ROW_TILE
    tiles_per_seq = 1 if sample else seq_len // tm
    grid = (rows // tm,)
    row = lambda i: (i, 0)
    slab_spec = pl.BlockSpec((SLABS, tm, V7X_LANES), lambda i: (0, i, 0))
    table_spec = pl.BlockSpec((tm, V7X_LANES), lambda i: (i % tiles_per_seq, 0))
    in_specs = [
        pl.BlockSpec((tm, D_MODEL), row),
        table_spec, table_spec,
        pl.BlockSpec((None, 1, D_MODEL), lambda i: (layer, 0, 0)),
        _resident((None, D_MODEL, IN_PROJ_WIDTH), lambda i: (layer, 0, 0)),
        pl.BlockSpec((None, CONV_WIDTH, CONV_CHANNELS), lambda i: (layer, 0, 0)),
    ]
    args = [x, cs, sn, gains, w_in, conv_w]
    slab_shape = jax.ShapeDtypeStruct((SLABS, rows, V7X_LANES), F32)
    out_shape = [slab_shape, slab_shape, slab_shape,
                 jax.ShapeDtypeStruct((rows, CONV_CHANNELS), BF16),
                 jax.ShapeDtypeStruct((rows, 2 * D_MODEL), BF16)]
    out_specs = [slab_spec, slab_spec, slab_spec,
                 pl.BlockSpec((tm, CONV_CHANNELS), row),
                 pl.BlockSpec((tm, 2 * D_MODEL), row)]
    scratch = []
    aliases = {}
    n_alias = 0
    if sample:
        in_specs += [pl.BlockSpec((tm, CONV_CHANNELS), row)] * 2
        args += list(prev)
        out_shape.append(jax.ShapeDtypeStruct((rows, CONV_CHANNELS), F32))
        out_specs.append(pl.BlockSpec((tm, CONV_CHANNELS), row))
    else:
        n_seq = rows // seq_len
        out_shape.append(jax.ShapeDtypeStruct((n_seq, V7X_SUBLANES, CONV_CHANNELS), F32))
        out_specs.append(pl.BlockSpec((None, V7X_SUBLANES, CONV_CHANNELS),
                                      lambda i: (i // tiles_per_seq, 0, 0)))
        if states is not None:
            n_alias = len(states)
            in_specs += [pl.BlockSpec(memory_space=pl.ANY)] * n_alias
            aliases = {len(args) + j: len(out_shape) + j for j in range(n_alias)}
            args += list(states)
        for window, _ in GROUPS:
            keep = min(window, seq_len)
            blk = min(keep, tm)
            first_tile = tiles_per_seq - keep // blk

            def state_index(i, first_tile=first_tile):
                return (layer, i // tiles_per_seq, 0, 0,
                        jnp.maximum(i % tiles_per_seq - first_tile, 0))

            for _ in range(2):
                out_shape.append(jax.ShapeDtypeStruct(
                    (DEPTH, n_seq, HEADS_PER_GROUP, HEAD_DIM, keep), F32))
                out_specs.append(pl.BlockSpec((None, None, HEADS_PER_GROUP, HEAD_DIM, blk), state_index))
        scratch.append(pltpu.VMEM((tm + V7X_SUBLANES, CONV_CHANNELS), F32))
    return pl.pallas_call(
        functools.partial(_inproj_kernel, sample, tm, tiles_per_seq, n_alias),
        out_shape=out_shape, grid=grid, in_specs=in_specs, out_specs=out_specs,
        scratch_shapes=scratch, input_output_aliases=aliases,
        name="inproj_sample" if sample else "inproj_prompt",
        compiler_params=pltpu.CompilerParams(
            dimension_semantics=("arbitrary",), vmem_limit_bytes=V7X_VMEM_LIMIT),
    )(*args)


def _deinterleave(src, dst, tmp, dil):
    assert dil in (1, STRIDE_STEP, STRIDE_STEP * STRIDE_STEP)
    for c in range(SLABS_PER_GROUP):
        if dil == 1:
            dst[c] = src[c]
            continue
        seg = SPAN // STRIDE_STEP
        stage = dst.at[c] if dil == STRIDE_STEP else tmp
        for r in range(STRIDE_STEP):
            stage[r * seg:(r + 1) * seg, :] = src[c, pl.ds(r, seg, stride=STRIDE_STEP), :]
        if dil == STRIDE_STEP:
            continue
        seg2 = seg // STRIDE_STEP
        for r in range(STRIDE_STEP):
            for r2 in range(STRIDE_STEP):
                res = r + STRIDE_STEP * r2
                dst[c, res * seg2:(res + 1) * seg2, :] = tmp[pl.ds(r * seg + r2, seg2, stride=STRIDE_STEP), :]


def _attn_kernel(dil, unroll, q_ref, k_ref, v_ref, o_ref, m_ref, l_ref, qd, kd, vd, tmp, bias):
    seg = SPAN // dil
    span_idx = pl.program_id(1)
    par = span_idx % 2
    first_span = span_idx == 0

    qi = lax.broadcasted_iota(jnp.int32, (2 * BLOCK, 2 * BLOCK), 0) % BLOCK
    kj = lax.broadcasted_iota(jnp.int32, (2 * BLOCK, 2 * BLOCK), 1)
    own = kj >= BLOCK
    slack = jnp.where(own, qi - (kj - BLOCK), kj - qi)
    bias[0] = jnp.where(slack >= 0, 0.0, NEG_INF)
    bias[1] = jnp.where(jnp.where(own, slack, -1) >= 0, 0.0, NEG_INF)
    low_q = lax.broadcasted_iota(jnp.int32, (BLOCK, V7X_LANES), 1) < HEAD_DIM
    low_kv = lax.broadcasted_iota(jnp.int32, (2 * BLOCK, V7X_LANES), 1) < HEAD_DIM

    @pl.when(first_span)
    def _():
        kd[1] = jnp.zeros(kd.shape[1:], F32)
        vd[1] = jnp.zeros(vd.shape[1:], F32)

    q_rows = q_ref
    if dil > 1:
        _deinterleave(q_ref, qd, tmp, dil)
        q_rows = qd
    _deinterleave(k_ref, kd.at[par], tmp, dil)
    _deinterleave(v_ref, vd.at[par], tmp, dil)

    def body(blk, carry):
        res = blk % dil
        sub = blk // dil
        base = pl.multiple_of(res * seg + sub * BLOCK, BLOCK)
        inner = sub > 0
        pbuf = jnp.where(inner, par, 1 - par)
        prow = pl.multiple_of(jnp.where(inner, base - BLOCK, res * seg + seg - BLOCK), BLOCK)
        no_prev = jnp.logical_and(first_span, sub == 0).astype(jnp.int32)
        row0 = res + dil * BLOCK * sub
        out_rows = pl.ds(row0, BLOCK, stride=dil) if dil > 1 else pl.ds(row0, BLOCK)
        for c in range(SLABS_PER_GROUP):
            q = q_rows[c, pl.ds(base, BLOCK), :]
            k2 = jnp.concatenate([kd[pbuf, c, pl.ds(prow, BLOCK), :],
                                  kd[par, c, pl.ds(base, BLOCK), :]], axis=0)
            v2 = jnp.concatenate([vd[pbuf, c, pl.ds(prow, BLOCK), :],
                                  vd[par, c, pl.ds(base, BLOCK), :]], axis=0)
            q2 = jnp.concatenate([jnp.where(low_q, q, 0.0), jnp.where(low_q, 0.0, q)], axis=0)
            s = lax.dot_general(q2.astype(BF16), k2.astype(BF16), (((1,), (1,)), ((), ())),
                                preferred_element_type=F32)
            s = s + bias[no_prev]
            m = jnp.max(s, axis=-1, keepdims=True)
            p = jnp.exp(s - m)
            l = jnp.sum(p, axis=-1, keepdims=True)
            pb = p.astype(BF16)
            p_cat = jnp.concatenate([pb[0:BLOCK], pb[BLOCK:2 * BLOCK]], axis=1)
            v_stack = jnp.concatenate([jnp.where(low_kv, v2, 0.0), jnp.where(low_kv, 0.0, v2)],
                                      axis=0).astype(BF16)
            o = _dot(p_cat, v_stack)
            l_slab = jnp.where(low_q, l[0:BLOCK], l[BLOCK:2 * BLOCK])
            m_slab = jnp.where(low_q, m[0:BLOCK], m[BLOCK:2 * BLOCK])
            o_ref[c, out_rows, :] = o
            m_ref[c, out_rows, :] = m_slab
            l_ref[c, out_rows, :] = l_slab
        return carry

    lax.fori_loop(0, SPAN // BLOCK, body, 0, unroll=unroll)


def _attn_group(group, q, k, v, n_seq, seq_len):
    dil = GROUPS[group][1]
    rows = q.shape[1]
    spans_per_seq = seq_len // SPAN
    cur_spec = pl.BlockSpec((SLABS_PER_GROUP, SPAN, V7X_LANES),
                            lambda b, n: (group, b * spans_per_seq + n, 0))
    out_spec = pl.BlockSpec((SLABS_PER_GROUP, SPAN, V7X_LANES),
                            lambda b, n: (0, b * spans_per_seq + n, 0))
    out = jax.ShapeDtypeStruct((SLABS_PER_GROUP, rows, V7X_LANES), F32)
    span_buf = (SLABS_PER_GROUP, SPAN, V7X_LANES)
    return pl.pallas_call(
        functools.partial(_attn_kernel, dil, ATTN_UNROLL),
        out_shape=[out, out, out], grid=(n_seq, spans_per_seq),
        in_specs=[cur_spec, cur_spec, cur_spec],
        out_specs=[out_spec, out_spec, out_spec],
        scratch_shapes=[pltpu.VMEM(span_buf, F32), pltpu.VMEM((2,) + span_buf, F32),
                        pltpu.VMEM((2,) + span_buf, F32), pltpu.VMEM((SPAN, V7X_LANES), F32),
                        pltpu.VMEM((2, 2 * BLOCK, 2 * BLOCK), F32)],
        name="attn_prompt_d%d" % dil,
        compiler_params=pltpu.CompilerParams(
            dimension_semantics=("arbitrary", "arbitrary"), vmem_limit_bytes=V7X_VMEM_LIMIT),
    )(q, k, v)


def _tail_kernel(merge, *refs):
    if merge:
        h_ref, o0, m0, l0, o1, m1, l1, o2, m2, l2 = refs[:10]
        rest = refs[10:]
    else:
        h_ref, attn_ref = refs[:2]
        rest = refs[2:]
    (cbc_ref, sg_ref, ple_ref, wa_ref, wc_ref, wo_ref, wu_ref, wd_ref, wg_ref, wp_ref,
     gains_ref, out_ref) = rest

    if merge:
        ms = [m0[...], m1[...], m2[...]]
        mx = jnp.maximum(jnp.maximum(ms[0], ms[1]), ms[2])
        es = [jnp.exp(m - mx) for m in ms]
        num = es[0] * o0[...] + es[1] * o1[...] + es[2] * o2[...]
        attn3 = num / (es[0] * l0[...] + es[1] * l1[...] + es[2] * l2[...])
        attn = jnp.concatenate([attn3[c] for c in range(SLABS_PER_GROUP)], axis=1)
    else:
        attn = attn_ref[...]

    h = h_ref[...]
    a = _dot(attn.astype(BF16), wa_ref[...])
    c = _dot(cbc_ref[...], wc_ref[...])
    sg = sg_ref[...].astype(F32)
    mix = sg[:, 0:D_MODEL] * a + sg[:, D_MODEL:2 * D_MODEL] * c
    h = h + _rms(_dot(mix.astype(BF16), wo_ref[...]), gains_ref[1:2, :])

    xn = _rms(h, gains_ref[2:3, :]).astype(BF16)
    acc = None
    for lo in range(0, D_FF, MLP_CHUNK):
        f = _dot(xn, wu_ref[:, lo:lo + MLP_CHUNK])
        f = jnp.square(jnp.maximum(f, 0.0)).astype(BF16)
        t = _dot(f, wd_ref[lo:lo + MLP_CHUNK, :])
        acc = t if acc is None else acc + t
    h = h + _rms(acc, gains_ref[3:4, :])

    gate = jax.nn.sigmoid(_dot(_rms(h, gains_ref[4:5, :]).astype(BF16), wg_ref[...]))
    out_ref[...] = h + gate * _dot(ple_ref[...].astype(BF16), wp_ref[...])


def _tail(layer, h, attn_parts, cbc, sg, ple, weights, gains):
    rows = h.shape[0]
    merge = len(attn_parts) > 1
    tm = ROW_TILE if merge else rows
    row = lambda i: (i, 0)
    in_specs = [pl.BlockSpec((tm, D_MODEL), row)]
    if merge:
        in_specs += [pl.BlockSpec((SLABS_PER_GROUP, tm, V7X_LANES), lambda i: (0, i, 0))] * 9
    else:
        in_specs += [pl.BlockSpec((tm, GROUP_WIDTH), row)]
    in_specs += [
        pl.BlockSpec((tm, CONV_CHANNELS), row),
        pl.BlockSpec((tm, 2 * D_MODEL), row),
        pl.BlockSpec((None, tm, PLE_DIM), lambda i: (layer, i, 0)),
    ]
    in_specs += [_resident((None,) + w.shape[1:], lambda i: (layer, 0, 0)) for w in weights]
    in_specs += [pl.BlockSpec((None,) + gains.shape[1:], lambda i: (layer, 0, 0))]
    return pl.pallas_call(
        functools.partial(_tail_kernel, merge),
        out_shape=jax.ShapeDtypeStruct((rows, D_MODEL), F32),
        grid=(rows // tm,), in_specs=in_specs,
        out_specs=pl.BlockSpec((tm, D_MODEL), row),
        name="tail_prompt" if merge else "tail_sample",
        compiler_params=pltpu.CompilerParams(
            dimension_semantics=("arbitrary",), vmem_limit_bytes=V7X_VMEM_LIMIT),
    )(h, *attn_parts, cbc, sg, ple, *weights, gains)


def _shift_in(old, new_col):
    width = old.shape[1]
    tiles = width // V7X_LANES
    lane = lax.broadcasted_iota(jnp.int32, (old.shape[0], V7X_LANES), 1)
    last = lane == V7X_LANES - 1
    rot = [pltpu.roll(old[:, t * V7X_LANES:(t + 1) * V7X_LANES], V7X_LANES - 1, 1)
           for t in range(tiles)]
    out = [jnp.where(last, rot[t + 1] if t + 1 < tiles else new_col, rot[t]) for t in range(tiles)]
    return out[0] if tiles == 1 else jnp.concatenate(out, axis=1)


def _sample_attn_kernel(n_alias, *refs):
    q_ref, kn_ref, vn_ref = refs[:3]
    caches = refs[3:3 + 2 * N_GROUPS]
    attn_ref = refs[3 + 2 * N_GROUPS + n_alias]
    outs = refs[4 + 2 * N_GROUPS + n_alias:]

    q = q_ref[...]
    kn = kn_ref[...]
    vn = vn_ref[...]
    head_row = lax.broadcasted_iota(jnp.int32, (V7X_SUBLANES, GROUP_WIDTH), 0)
    head_lane = lax.broadcasted_iota(jnp.int32, (V7X_SUBLANES, GROUP_WIDTH), 1) // HEAD_DIM
    own_head = head_row == head_lane

    parts = []
    for g, (window, dil) in enumerate(GROUPS):
        sl = slice(g * GROUP_WIDTH, (g + 1) * GROUP_WIDTH)
        kt = caches[2 * g][...].reshape(GROUP_WIDTH, window)
        vt = caches[2 * g + 1][...].reshape(GROUP_WIDTH, window)
        q_heads = jnp.where(own_head, jnp.broadcast_to(q[:, sl], own_head.shape), 0.0)
        s = _dot(q_heads.astype(BF16), kt.astype(BF16))
        pos = lax.broadcasted_iota(jnp.int32, s.shape, 1)
        s = jnp.where(pos % dil == 0, s, NEG_INF)
        s_new = jnp.sum(q_heads * kn[:, sl], axis=-1, keepdims=True)
        m = jnp.maximum(jnp.max(s, axis=-1, keepdims=True), s_new)
        p = jnp.exp(s - m)
        p_new = jnp.exp(s_new - m)
        l = jnp.sum(p, axis=-1, keepdims=True) + p_new
        o = lax.dot_general(p.astype(BF16), vt.astype(BF16), (((1,), (1,)), ((), ())),
                            preferred_element_type=F32) + p_new * vn[:, sl]
        parts.append((o / l, m, l))

        k_col = jnp.transpose(jnp.broadcast_to(kn[:, sl], (V7X_LANES, GROUP_WIDTH)))
        v_col = jnp.transpose(jnp.broadcast_to(vn[:, sl], (V7X_LANES, GROUP_WIDTH)))
        outs[2 * g][...] = _shift_in(kt, k_col).reshape(HEADS_PER_GROUP, HEAD_DIM, window)
        outs[2 * g + 1][...] = _shift_in(vt, v_col).reshape(HEADS_PER_GROUP, HEAD_DIM, window)

    mx = jnp.maximum(jnp.maximum(parts[0][1], parts[1][1]), parts[2][1])
    ws = [l * jnp.exp(m - mx) for _, m, l in parts]
    num = ws[0] * parts[0][0] + ws[1] * parts[1][0] + ws[2] * parts[2][0]
    merged = num / (ws[0] + ws[1] + ws[2])
    attn_ref[...] = jnp.sum(jnp.where(own_head, merged, 0.0), axis=0, keepdims=True)


def _sample_attn(layer, q, k_new, v_new, caches_t, new_caches):
    batch = q.shape[0]
    vec_spec = pl.BlockSpec((None, 1, ATTN_WIDTH), lambda b: (b, 0, 0))
    in_specs = [vec_spec, vec_spec, vec_spec]
    cache_specs = [pl.BlockSpec((None, None) + c.shape[2:], lambda b: (layer, b, 0, 0, 0))
                   for c in caches_t]
    in_specs += cache_specs
    args = [q, k_new, v_new, *caches_t]
    aliases = {}
    n_alias = 0
    if new_caches is not None:
        n_alias = len(new_caches)
        in_specs += [pl.BlockSpec(memory_space=pl.ANY)] * n_alias
        aliases = {len(args) + j: 1 + j for j in range(n_alias)}
        args += list(new_caches)
    out_shape = [jax.ShapeDtypeStruct((batch, 1, GROUP_WIDTH), F32)]
    out_shape += [jax.ShapeDtypeStruct(c.shape, F32) for c in caches_t]
    out_specs = [pl.BlockSpec((None, 1, GROUP_WIDTH), lambda b: (b, 0, 0))] + cache_specs
    res = pl.pallas_call(
        functools.partial(_sample_attn_kernel, n_alias),
        out_shape=out_shape, grid=(batch,), in_specs=in_specs, out_specs=out_specs,
        input_output_aliases=aliases, name="attn_sample",
        compiler_params=pltpu.CompilerParams(
            dimension_semantics=("arbitrary",), vmem_limit_bytes=V7X_VMEM_LIMIT),
    )(*args)
    return res[0], res[1:]


def _rope_tables(pos):
    half = HEAD_DIM // 2
    inv = jnp.power(ROPE_THETA, -2.0 * jnp.arange(half, dtype=F32) / HEAD_DIM)
    ang = pos.astype(F32)[:, None] * inv[None, :]
    cos, sin = jnp.cos(ang), jnp.sin(ang)
    reps = V7X_LANES // HEAD_DIM
    return (jnp.concatenate([cos, cos] * reps, axis=1), jnp.concatenate([-sin, sin] * reps, axis=1))


def kernel(x_prompt, x_sample, p_prompt, p_sample, cache_k_w128, cache_v_w128, cache_k_w512, cache_v_w512, cache_k_w2048, cache_v_w2048, state_conv, w_in, conv_w, w_attn_out, w_conv_out, w_o, g_pre_mix, g_post_mix, w_up, w_down, g_pre_mlp, g_post_mlp, g_ple, w_ple_gate, w_ple_proj):
    n_seq, seq_len, _ = x_prompt.shape
    n_dec = x_sample.shape[0]
    rows = n_seq * seq_len

    w_in_b = w_in.astype(BF16)
    tail_w = [w.astype(BF16) for w in (w_attn_out, w_conv_out, w_o, w_up, w_down, w_ple_gate, w_ple_proj)]
    gains = jnp.stack([g_pre_mix, g_post_mix, g_pre_mlp, g_post_mlp, g_ple], axis=1)
    g_in = g_pre_mix[:, None, :]

    cs_p, sn_p = _rope_tables(jnp.arange(seq_len, dtype=jnp.int32))
    cs_s, sn_s = _rope_tables(jnp.full((n_dec,), PAST_LEN, dtype=jnp.int32))

    caches = (cache_k_w128, cache_v_w128, cache_k_w512, cache_v_w512, cache_k_w2048, cache_v_w2048)
    caches_t = [jnp.transpose(c, (0, 1, 3, 4, 2)) for c in caches]

    hp = x_prompt.reshape(rows, D_MODEL)
    hs = x_sample.reshape(n_dec, D_MODEL)
    ple_p = p_prompt.reshape(DEPTH, rows, PLE_DIM)
    ple_s = p_sample.reshape(DEPTH, n_dec, PLE_DIM)

    conv_p, conv_s = [], []
    states_p = None
    new_caches = None
    for i in range(DEPTH):
        res = _inproj(i, hp, cs_p, sn_p, g_in, w_in_b, conv_w, seq_len=seq_len, states=states_p)
        q, k, v, cbc, sg, cst = res[:6]
        states_p = res[6:]
        parts = []
        for g in range(N_GROUPS):
            parts += _attn_group(g, q, k, v, n_seq, seq_len)
        hp = _tail(i, hp, parts, cbc, sg, ple_p, tail_w, gains)
        conv_p.append(cst[:, V7X_SUBLANES - (CONV_WIDTH - 1):, :])

        prev = (state_conv[i, :, 0, :], state_conv[i, :, 1, :])
        qs, ks, vs, cbcs, sgs, us = _inproj(i, hs, cs_s, sn_s, g_in, w_in_b, conv_w, prev=prev)
        as_rows = lambda t: jnp.transpose(t, (1, 0, 2)).reshape(n_dec, 1, ATTN_WIDTH)
        attn_s, new_caches = _sample_attn(i, as_rows(qs), as_rows(ks), as_rows(vs), caches_t, new_caches)
        hs = _tail(i, hs, [attn_s.reshape(n_dec, GROUP_WIDTH)], cbcs, sgs, ple_s, tail_w, gains)
        conv_s.append(jnp.stack([prev[1], us], axis=1))

    to_rows = lambda c: jnp.transpose(c, (0, 1, 4, 2, 3))
    return (hp.reshape(n_seq, seq_len, D_MODEL), hs.reshape(n_dec, 1, D_MODEL),
            *[to_rows(c) for c in states_p], jnp.stack(conv_p),
            *[to_rows(c) for c in new_caches], jnp.stack(conv_s))
```

```python
import functools

import jax
import jax.numpy as jnp
from jax import lax
from jax.experimental import pallas as pl
from jax.experimental.pallas import tpu as pltpu

F32 = jnp.float32
BF16 = jnp.bfloat16

D_MODEL = 1024
DEPTH = 4
PAST_LEN = 8192
HEAD_DIM = 64
HEADS_PER_GROUP = 4
GROUPS = ((128, 1), (512, 4), (2048, 16))
N_GROUPS = len(GROUPS)
GROUP_WIDTH = HEADS_PER_GROUP * HEAD_DIM
ATTN_WIDTH = N_GROUPS * GROUP_WIDTH
CONV_CHANNELS = 768
CONV_WIDTH = 3
D_FF = 4 * D_MODEL
PLE_DIM = 256
ROPE_THETA = 10000.0
BLOCK = 128
RMS_EPS = 1e-6
NEG_INF = -1e30
IN_PROJ_WIDTH = 3 * ATTN_WIDTH + 3 * CONV_CHANNELS + 2 * D_MODEL
OFF_Q, OFF_K, OFF_V = 0, ATTN_WIDTH, 2 * ATTN_WIDTH
OFF_CB = 3 * ATTN_WIDTH
OFF_CC = OFF_CB + CONV_CHANNELS
OFF_CH = OFF_CC + CONV_CHANNELS
OFF_GA = OFF_CH + CONV_CHANNELS
OFF_GB = OFF_GA + D_MODEL

V7X_LANES = 128
V7X_SUBLANES = 8
V7X_VMEM_LIMIT = 56 * 1024 * 1024
SLABS = ATTN_WIDTH // V7X_LANES
SLABS_PER_GROUP = GROUP_WIDTH // V7X_LANES
HEADS_PER_SLAB = V7X_LANES // HEAD_DIM
ROW_TILE = 512
TAIL_ROW_TILE = 512
SPAN = BLOCK * GROUPS[-1][1]
MLP_CHUNK = 1024
ROW_PARTS = 2
STRIDE_STEP = 4
ATTN_UNROLL = 4


def _rms(x, g):
    return x * lax.rsqrt(jnp.mean(x * x, axis=-1, keepdims=True) + RMS_EPS) * g


def _dot(a, b):
    return jnp.dot(a, b, preferred_element_type=F32)


def _resident(shape, index):
    return pl.BlockSpec(shape, index, pipeline_mode=pl.Buffered(1))


def _inproj_kernel(sample, tm, tiles_per_seq, n_alias, *refs):
    if sample:
        (x_ref, cs_ref, sn_ref, g_ref, w_ref, cw_ref, p0_ref, p1_ref,
         q_ref, k_ref, v_ref, cbc_ref, sg_ref, u_ref) = refs
    else:
        x_ref, cs_ref, sn_ref, g_ref, w_ref, cw_ref = refs[:6]
        q_ref, k_ref, v_ref, cbc_ref, sg_ref, cst_ref = refs[6 + n_alias:12 + n_alias]
        state_refs = refs[12 + n_alias:12 + n_alias + 2 * N_GROUPS]
        u_scr = refs[-1]

    parts = 1 if sample else ROW_PARTS
    pr = tm // parts
    halo = V7X_SUBLANES
    lane = lax.broadcasted_iota(jnp.int32, (pr, V7X_LANES), 1)
    lower = (lane % HEAD_DIM) < (HEAD_DIM // 2)
    cw = cw_ref[...]

    if not sample:
        @pl.when(pl.program_id(0) % tiles_per_seq == 0)
        def _():
            u_scr[0:halo, :] = jnp.zeros((halo, CONV_CHANNELS), F32)

    xns, cbs = [], []
    for i in range(parts):
        rows = slice(i * pr, (i + 1) * pr)
        xn = _rms(x_ref[rows, :], g_ref[...]).astype(BF16)
        xns.append(xn)
        cs = cs_ref[rows, :]
        sn = sn_ref[rows, :]

        def proj(lo, width, xn=xn):
            return _dot(xn, w_ref[:, lo:lo + width])

        def rope_store(t, out_ref, scale, rows=rows, cs=cs, sn=sn):
            for c in range(SLABS):
                xc = t[:, c * V7X_LANES:(c + 1) * V7X_LANES]
                sw = jnp.where(lower, pltpu.roll(xc, V7X_LANES - HEAD_DIM // 2, 1),
                               pltpu.roll(xc, HEAD_DIM // 2, 1))
                r = xc * cs + sw * sn
                if scale is not None:
                    r = r * scale
                out_ref[c, rows, :] = r

        rope_store(proj(OFF_Q, ATTN_WIDTH), q_ref, HEAD_DIM ** -0.5)
        rope_store(proj(OFF_K, ATTN_WIDTH), k_ref, None)
        vv = proj(OFF_V, ATTN_WIDTH)
        for c in range(SLABS):
            v_ref[c, rows, :] = vv[:, c * V7X_LANES:(c + 1) * V7X_LANES]
        cbs.append(proj(OFF_CB, CONV_CHANNELS))
        u = proj(OFF_CC, CONV_CHANNELS) * proj(OFF_CH, CONV_CHANNELS)
        if sample:
            u_ref[...] = u
        else:
            u_scr[halo + i * pr:halo + (i + 1) * pr, :] = u

    for i in range(parts):
        rows = slice(i * pr, (i + 1) * pr)
        if sample:
            conv = cw[0:1] * p0_ref[...] + cw[1:2] * p1_ref[...] + cw[2:3] * u_ref[...]
        else:
            lo = halo + i * pr
            conv = (cw[0:1] * u_scr[lo - 2:lo + pr - 2, :] + cw[1:2] * u_scr[lo - 1:lo + pr - 1, :]
                    + cw[2:3] * u_scr[lo:lo + pr, :])
        cbc_ref[rows, :] = (cbs[i] * conv).astype(BF16)
        for lo, off in ((0, OFF_GA), (D_MODEL, OFF_GB)):
            gate = jax.nn.sigmoid(_dot(xns[i], w_ref[:, off:off + D_MODEL]))
            sg_ref[rows, lo:lo + D_MODEL] = gate.astype(BF16)

    if not sample:
        tail = u_scr[tm:tm + halo, :]
        u_scr[0:halo, :] = tail
        cst_ref[...] = tail
        for g, (window, _) in enumerate(GROUPS):
            keep = min(window, tm)
            for j, src in enumerate((k_ref, v_ref)):
                for half in range(SLABS_PER_GROUP):
                    last = src[g * SLABS_PER_GROUP + half, tm - keep:tm, :]
                    heads = slice(half * HEADS_PER_SLAB, (half + 1) * HEADS_PER_SLAB)
                    state_refs[2 * g + j][heads, :, :] = jnp.transpose(last).reshape(
                        HEADS_PER_SLAB, HEAD_DIM, keep)


def _inproj(layer, x, cs, sn, gains, w_in, conv_w, *, seq_len=None, prev=None, states=None):
    rows = x.shape[0]
    sample = prev is not None
    tm = rows if sample else ROW_TILE
    tiles_per_seq = 1 if sample else seq_len // tm
    grid = (rows // tm,)
    row = lambda i: (i, 0)
    slab_spec = pl.BlockSpec((SLABS, tm, V7X_LANES), lambda i: (0, i, 0))
    table_spec = pl.BlockSpec((tm, V7X_LANES), lambda i: (i % tiles_per_seq, 0))
    in_specs = [
        pl.BlockSpec((tm, D_MODEL), row),
        table_spec, table_spec,
        pl.BlockSpec((None, 1, D_MODEL), lambda i: (layer, 0, 0)),
        _resident((None, D_MODEL, IN_PROJ_WIDTH), lambda i: (layer, 0, 0)),
        pl.BlockSpec((None, CONV_WIDTH, CONV_CHANNELS), lambda i: (layer, 0, 0)),
    ]
    args = [x, cs, sn, gains, w_in, conv_w]
    slab_shape = jax.ShapeDtypeStruct((SLABS, rows, V7X_LANES), F32)
    out_shape = [slab_shape, slab_shape, slab_shape,
                 jax.ShapeDtypeStruct((rows, CONV_CHANNELS), BF16),
                 jax.ShapeDtypeStruct((rows, 2 * D_MODEL), BF16)]
    out_specs = [slab_spec, slab_spec, slab_spec,
                 pl.BlockSpec((tm, CONV_CHANNELS), row),
                 pl.BlockSpec((tm, 2 * D_MODEL), row)]
    scratch = []
    aliases = {}
    n_alias = 0
    if sample:
        in_specs += [pl.BlockSpec((tm, CONV_CHANNELS), row)] * 2
        args += list(prev)
        out_shape.append(jax.ShapeDtypeStruct((rows, CONV_CHANNELS), F32))
        out_specs.append(pl.BlockSpec((tm, CONV_CHANNELS), row))
    else:
        n_seq = rows // seq_len
        out_shape.append(jax.ShapeDtypeStruct((n_seq, V7X_SUBLANES, CONV_CHANNELS), F32))
        out_specs.append(pl.BlockSpec((None, V7X_SUBLANES, CONV_CHANNELS),
                                      lambda i: (i // tiles_per_seq, 0, 0)))
        if states is not None:
            n_alias = len(states)
            in_specs += [pl.BlockSpec(memory_space=pl.ANY)] * n_alias
            aliases = {len(args) + j: len(out_shape) + j for j in range(n_alias)}
            args += list(states)
        for window, _ in GROUPS:
            keep = min(window, seq_len)
            blk = min(keep, tm)
            first_tile = tiles_per_seq - keep // blk

            def state_index(i, first_tile=first_tile):
                return (layer, i // tiles_per_seq, 0, 0,
                        jnp.maximum(i % tiles_per_seq - first_tile, 0))

            for _ in range(2):
                out_shape.append(jax.ShapeDtypeStruct(
                    (DEPTH, n_seq, HEADS_PER_GROUP, HEAD_DIM, keep), F32))
                out_specs.append(pl.BlockSpec((None, None, HEADS_PER_GROUP, HEAD_DIM, blk), state_index))
        scratch.append(pltpu.VMEM((tm + V7X_SUBLANES, CONV_CHANNELS), F32))
    return pl.pallas_call(
        functools.partial(_inproj_kernel, sample, tm, tiles_per_seq, n_alias),
        out_shape=out_shape, grid=grid, in_specs=in_specs, out_specs=out_specs,
        scratch_shapes=scratch, input_output_aliases=aliases,
        name="inproj_sample" if sample else "inproj_prompt",
        compiler_params=pltpu.CompilerParams(
            dimension_semantics=("arbitrary",), vmem_limit_bytes=V7X_VMEM_LIMIT),
    )(*args)


def _deinterleave(src, dst, tmp, dil):
    assert dil in (1, STRIDE_STEP, STRIDE_STEP * STRIDE_STEP)
    for c in range(SLABS_PER_GROUP):
        if dil == 1:
            dst[c] = src[c]
            continue
        seg = SPAN // STRIDE_STEP
        stage = dst.at[c] if dil == STRIDE_STEP else tmp
        for r in range(STRIDE_STEP):
            stage[r * seg:(r + 1) * seg, :] = src[c, pl.ds(r, seg, stride=STRIDE_STEP), :]
        if dil == STRIDE_STEP:
            continue
        seg2 = seg // STRIDE_STEP
        for r in range(STRIDE_STEP):
            for r2 in range(STRIDE_STEP):
                res = r + STRIDE_STEP * r2
                dst[c, res * seg2:(res + 1) * seg2, :] = tmp[pl.ds(r * seg + r2, seg2, stride=STRIDE_STEP), :]


def _attn_kernel(dil, unroll, q_ref, k_ref, v_ref, o_ref, m_ref, l_ref, qd, kd, vd, tmp, bias):
    seg = SPAN // dil
    span_idx = pl.program_id(1)
    par = span_idx % 2
    first_span = span_idx == 0

    qi = lax.broadcasted_iota(jnp.int32, (2 * BLOCK, 2 * BLOCK), 0) % BLOCK
    kj = lax.broadcasted_iota(jnp.int32, (2 * BLOCK, 2 * BLOCK), 1)
    own = kj >= BLOCK
    slack = jnp.where(own, qi - (kj - BLOCK), kj - qi)
    bias[0] = jnp.where(slack >= 0, 0.0, NEG_INF)
    bias[1] = jnp.where(jnp.where(own, slack, -1) >= 0, 0.0, NEG_INF)
    low_q = lax.broadcasted_iota(jnp.int32, (BLOCK, V7X_LANES), 1) < HEAD_DIM
    low_kv = lax.broadcasted_iota(jnp.int32, (2 * BLOCK, V7X_LANES), 1) < HEAD_DIM

    @pl.when(first_span)
    def _():
        kd[1] = jnp.zeros(kd.shape[1:], F32)
        vd[1] = jnp.zeros(vd.shape[1:], F32)

    q_rows = q_ref
    if dil > 1:
        _deinterleave(q_ref, qd, tmp, dil)
        q_rows = qd
    _deinterleave(k_ref, kd.at[par], tmp, dil)
    _deinterleave(v_ref, vd.at[par], tmp, dil)

    def body(blk, carry):
        res = blk % dil
        sub = blk // dil
        base = pl.multiple_of(res * seg + sub * BLOCK, BLOCK)
        inner = sub > 0
        pbuf = jnp.where(inner, par, 1 - par)
        prow = pl.multiple_of(jnp.where(inner, base - BLOCK, res * seg + seg - BLOCK), BLOCK)
        no_prev = jnp.logical_and(first_span, sub == 0).astype(jnp.int32)
        row0 = res + dil * BLOCK * sub
        out_rows = pl.ds(row0, BLOCK, stride=dil) if dil > 1 else pl.ds(row0, BLOCK)
        for c in range(SLABS_PER_GROUP):
            q = q_rows[c, pl.ds(base, BLOCK), :]
            k2 = jnp.concatenate([kd[pbuf, c, pl.ds(prow, BLOCK), :],
                                  kd[par, c, pl.ds(base, BLOCK), :]], axis=0)
            v2 = jnp.concatenate([vd[pbuf, c, pl.ds(prow, BLOCK), :],
                                  vd[par, c, pl.ds(base, BLOCK), :]], axis=0)
            q2 = jnp.concatenate([jnp.where(low_q, q, 0.0), jnp.where(low_q, 0.0, q)], axis=0)
            s = lax.dot_general(q2.astype(BF16), k2.astype(BF16), (((1,), (1,)), ((), ())),
                                preferred_element_type=F32)
            s = s + bias[no_prev]
            m = jnp.max(s, axis=-1, keepdims=True)
            p = jnp.exp(s - m)
            l = jnp.sum(p, axis=-1, keepdims=True)
            pb = p.astype(BF16)
            p_cat = jnp.concatenate([pb[0:BLOCK], pb[BLOCK:2 * BLOCK]], axis=1)
            v_stack = jnp.concatenate([jnp.where(low_kv, v2, 0.0), jnp.where(low_kv, 0.0, v2)],
                                      axis=0).astype(BF16)
            o = _dot(p_cat, v_stack)
            l_slab = jnp.where(low_q, l[0:BLOCK], l[BLOCK:2 * BLOCK])
            m_slab = jnp.where(low_q, m[0:BLOCK], m[BLOCK:2 * BLOCK])
            o_ref[c, out_rows, :] = o
            m_ref[c, out_rows, :] = m_slab
            l_ref[c, out_rows, :] = l_slab
        return carry

    lax.fori_loop(0, SPAN // BLOCK, body, 0, unroll=unroll)


def _attn_group(group, q, k, v, n_seq, seq_len):
    dil = GROUPS[group][1]
    rows = q.shape[1]
    spans_per_seq = seq_len // SPAN
    cur_spec = pl.BlockSpec((SLABS_PER_GROUP, SPAN, V7X_LANES),
                            lambda b, n: (group, b * spans_per_seq + n, 0))
    out_spec = pl.BlockSpec((SLABS_PER_GROUP, SPAN, V7X_LANES),
                            lambda b, n: (0, b * spans_per_seq + n, 0))
    out = jax.ShapeDtypeStruct((SLABS_PER_GROUP, rows, V7X_LANES), F32)
    span_buf = (SLABS_PER_GROUP, SPAN, V7X_LANES)
    return pl.pallas_call(
        functools.partial(_attn_kernel, dil, ATTN_UNROLL),
        out_shape=[out, out, out], grid=(n_seq, spans_per_seq),
        in_specs=[cur_spec, cur_spec, cur_spec],
        out_specs=[out_spec, out_spec, out_spec],
        scratch_shapes=[pltpu.VMEM(span_buf, F32), pltpu.VMEM((2,) + span_buf, F32),
                        pltpu.VMEM((2,) + span_buf, F32), pltpu.VMEM((SPAN, V7X_LANES), F32),
                        pltpu.VMEM((2, 2 * BLOCK, 2 * BLOCK), F32)],
        name="attn_prompt_d%d" % dil,
        compiler_params=pltpu.CompilerParams(
            dimension_semantics=("arbitrary", "arbitrary"), vmem_limit_bytes=V7X_VMEM_LIMIT),
    )(q, k, v)


def _tail_kernel(merge, parts, *refs):
    if merge:
        h_ref, o0, m0, l0, o1, m1, l1, o2, m2, l2 = refs[:10]
        rest = refs[10:]
    else:
        h_ref, attn_ref = refs[:2]
        rest = refs[2:]
    (cbc_ref, sg_ref, ple_ref, wa_ref, wc_ref, wo_ref, wu_ref, wd_ref, wg_ref, wp_ref,
     gains_ref, out_ref) = rest

    pr = h_ref.shape[0] // parts
    row_slices = [slice(i * pr, (i + 1) * pr) for i in range(parts)]

    mixed = []
    for rows in row_slices:
        if merge:
            ms = [m[:, rows, :] for m in (m0, m1, m2)]
            mx = jnp.maximum(jnp.maximum(ms[0], ms[1]), ms[2])
            es = [jnp.exp(m - mx) for m in ms]
            num = es[0] * o0[:, rows, :] + es[1] * o1[:, rows, :] + es[2] * o2[:, rows, :]
            attn3 = num / (es[0] * l0[:, rows, :] + es[1] * l1[:, rows, :] + es[2] * l2[:, rows, :])
            attn = jnp.concatenate([attn3[c] for c in range(SLABS_PER_GROUP)], axis=1)
        else:
            attn = attn_ref[rows, :]
        a = _dot(attn.astype(BF16), wa_ref[...])
        c = _dot(cbc_ref[rows, :], wc_ref[...])
        sg = sg_ref[rows, :].astype(F32)
        mix = sg[:, 0:D_MODEL] * a + sg[:, D_MODEL:2 * D_MODEL] * c
        mixed.append(_dot(mix.astype(BF16), wo_ref[...]))

    for rows, mo in zip(row_slices, mixed):
        out_ref[rows, :] = h_ref[rows, :] + _rms(mo, gains_ref[1:2, :])

    mlp = []
    for rows in row_slices:
        xn = _rms(out_ref[rows, :], gains_ref[2:3, :]).astype(BF16)
        acc = None
        for lo in range(0, D_FF, MLP_CHUNK):
            f = _dot(xn, wu_ref[:, lo:lo + MLP_CHUNK])
            f = jnp.square(jnp.maximum(f, 0.0)).astype(BF16)
            t = _dot(f, wd_ref[lo:lo + MLP_CHUNK, :])
            acc = t if acc is None else acc + t
        mlp.append(acc)

    for rows, acc in zip(row_slices, mlp):
        out_ref[rows, :] = out_ref[rows, :] + _rms(acc, gains_ref[3:4, :])

    for rows in row_slices:
        h = out_ref[rows, :]
        gate = jax.nn.sigmoid(_dot(_rms(h, gains_ref[4:5, :]).astype(BF16), wg_ref[...]))
        out_ref[rows, :] = h + gate * _dot(ple_ref[rows, :].astype(BF16), wp_ref[...])


def _tail(layer, h, attn_parts, cbc, sg, ple, weights, gains):
    rows = h.shape[0]
    merge = len(attn_parts) > 1
    tm = TAIL_ROW_TILE if merge else rows
    row = lambda i: (i, 0)
    in_specs = [pl.BlockSpec((tm, D_MODEL), row)]
    if merge:
        in_specs += [pl.BlockSpec((SLABS_PER_GROUP, tm, V7X_LANES), lambda i: (0, i, 0))] * 9
    else:
        in_specs += [pl.BlockSpec((tm, GROUP_WIDTH), row)]
    in_specs += [
        pl.BlockSpec((tm, CONV_CHANNELS), row),
        pl.BlockSpec((tm, 2 * D_MODEL), row),
        pl.BlockSpec((None, tm, PLE_DIM), lambda i: (layer, i, 0)),
    ]
    in_specs += [_resident((None,) + w.shape[1:], lambda i: (layer, 0, 0)) for w in weights]
    in_specs += [pl.BlockSpec((None,) + gains.shape[1:], lambda i: (layer, 0, 0))]
    return pl.pallas_call(
        functools.partial(_tail_kernel, merge, 1),
        out_shape=jax.ShapeDtypeStruct((rows, D_MODEL), F32),
        grid=(rows // tm,), in_specs=in_specs,
        out_specs=pl.BlockSpec((tm, D_MODEL), row),
        name="tail_prompt" if merge else "tail_sample",
        compiler_params=pltpu.CompilerParams(
            dimension_semantics=("arbitrary",), vmem_limit_bytes=V7X_VMEM_LIMIT),
    )(h, *attn_parts, cbc, sg, ple, *weights, gains)


def _shift_in(old, new_col):
    width = old.shape[1]
    tiles = width // V7X_LANES
    lane = lax.broadcasted_iota(jnp.int32, (old.shape[0], V7X_LANES), 1)
    last = lane == V7X_LANES - 1
    rot = [pltpu.roll(old[:, t * V7X_LANES:(t + 1) * V7X_LANES], V7X_LANES - 1, 1)
           for t in range(tiles)]
    out = [jnp.where(last, rot[t + 1] if t + 1 < tiles else new_col, rot[t]) for t in range(tiles)]
    return out[0] if tiles == 1 else jnp.concatenate(out, axis=1)


def _sample_attn_kernel(n_alias, *refs):
    q_ref, kn_ref, vn_ref = refs[:3]
    caches = refs[3:3 + 2 * N_GROUPS]
    attn_ref = refs[3 + 2 * N_GROUPS + n_alias]
    outs = refs[4 + 2 * N_GROUPS + n_alias:]

    q = q_ref[...]
    kn = kn_ref[...]
    vn = vn_ref[...]
    head_row = lax.broadcasted_iota(jnp.int32, (V7X_SUBLANES, GROUP_WIDTH), 0)
    head_lane = lax.broadcasted_iota(jnp.int32, (V7X_SUBLANES, GROUP_WIDTH), 1) // HEAD_DIM
    own_head = head_row == head_lane

    parts = []
    for g, (window, dil) in enumerate(GROUPS):
        sl = slice(g * GROUP_WIDTH, (g + 1) * GROUP_WIDTH)
        kt = caches[2 * g][...].reshape(GROUP_WIDTH, window)
        vt = caches[2 * g + 1][...].reshape(GROUP_WIDTH, window)
        q_heads = jnp.where(own_head, jnp.broadcast_to(q[:, sl], own_head.shape), 0.0)
        s = _dot(q_heads.astype(BF16), kt.astype(BF16))
        pos = lax.broadcasted_iota(jnp.int32, s.shape, 1)
        s = jnp.where(pos % dil == 0, s, NEG_INF)
        s_new = jnp.sum(q_heads * kn[:, sl], axis=-1, keepdims=True)
        m = jnp.maximum(jnp.max(s, axis=-1, keepdims=True), s_new)
        p = jnp.exp(s - m)
        p_new = jnp.exp(s_new - m)
        l = jnp.sum(p, axis=-1, keepdims=True) + p_new
        o = lax.dot_general(p.astype(BF16), vt.astype(BF16), (((1,), (1,)), ((), ())),
                            preferred_element_type=F32) + p_new * vn[:, sl]
        parts.append((o / l, m, l))

        k_col = jnp.transpose(jnp.broadcast_to(kn[:, sl], (V7X_LANES, GROUP_WIDTH)))
        v_col = jnp.transpose(jnp.broadcast_to(vn[:, sl], (V7X_LANES, GROUP_WIDTH)))
        outs[2 * g][...] = _shift_in(kt, k_col).reshape(HEADS_PER_GROUP, HEAD_DIM, window)
        outs[2 * g + 1][...] = _shift_in(vt, v_col).reshape(HEADS_PER_GROUP, HEAD_DIM, window)

    mx = jnp.maximum(jnp.maximum(parts[0][1], parts[1][1]), parts[2][1])
    ws = [l * jnp.exp(m - mx) for _, m, l in parts]
    num = ws[0] * parts[0][0] + ws[1] * parts[1][0] + ws[2] * parts[2][0]
    merged = num / (ws[0] + ws[1] + ws[2])
    attn_ref[...] = jnp.sum(jnp.where(own_head, merged, 0.0), axis=0, keepdims=True)


def _sample_attn(layer, q, k_new, v_new, caches_t, new_caches):
    batch = q.shape[0]
    vec_spec = pl.BlockSpec((None, 1, ATTN_WIDTH), lambda b: (b, 0, 0))
    in_specs = [vec_spec, vec_spec, vec_spec]
    cache_specs = [pl.BlockSpec((None, None) + c.shape[2:], lambda b: (layer, b, 0, 0, 0))
                   for c in caches_t]
    in_specs += cache_specs
    args = [q, k_new, v_new, *caches_t]
    aliases = {}
    n_alias = 0
    if new_caches is not None:
        n_alias = len(new_caches)
        in_specs += [pl.BlockSpec(memory_space=pl.ANY)] * n_alias
        aliases = {len(args) + j: 1 + j for j in range(n_alias)}
        args += list(new_caches)
    out_shape = [jax.ShapeDtypeStruct((batch, 1, GROUP_WIDTH), F32)]
    out_shape += [jax.ShapeDtypeStruct(c.shape, F32) for c in caches_t]
    out_specs = [pl.BlockSpec((None, 1, GROUP_WIDTH), lambda b: (b, 0, 0))] + cache_specs
    res = pl.pallas_call(
        functools.partial(_sample_attn_kernel, n_alias),
        out_shape=out_shape, grid=(batch,), in_specs=in_specs, out_specs=out_specs,
        input_output_aliases=aliases, name="attn_sample",
        compiler_params=pltpu.CompilerParams(
            dimension_semantics=("arbitrary",), vmem_limit_bytes=V7X_VMEM_LIMIT),
    )(*args)
    return res[0], res[1:]


def _rope_tables(pos):
    half = HEAD_DIM // 2
    inv = jnp.power(ROPE_THETA, -2.0 * jnp.arange(half, dtype=F32) / HEAD_DIM)
    ang = pos.astype(F32)[:, None] * inv[None, :]
    cos, sin = jnp.cos(ang), jnp.sin(ang)
    reps = V7X_LANES // HEAD_DIM
    return (jnp.concatenate([cos, cos] * reps, axis=1), jnp.concatenate([-sin, sin] * reps, axis=1))


def kernel(x_prompt, x_sample, p_prompt, p_sample, cache_k_w128, cache_v_w128, cache_k_w512, cache_v_w512, cache_k_w2048, cache_v_w2048, state_conv, w_in, conv_w, w_attn_out, w_conv_out, w_o, g_pre_mix, g_post_mix, w_up, w_down, g_pre_mlp, g_post_mlp, g_ple, w_ple_gate, w_ple_proj):
    n_seq, seq_len, _ = x_prompt.shape
    n_dec = x_sample.shape[0]
    rows = n_seq * seq_len

    w_in_b = w_in.astype(BF16)
    tail_w = [w.astype(BF16) for w in (w_attn_out, w_conv_out, w_o, w_up, w_down, w_ple_gate, w_ple_proj)]
    gains = jnp.stack([g_pre_mix, g_post_mix, g_pre_mlp, g_post_mlp, g_ple], axis=1)
    g_in = g_pre_mix[:, None, :]

    cs_p, sn_p = _rope_tables(jnp.arange(seq_len, dtype=jnp.int32))
    cs_s, sn_s = _rope_tables(jnp.full((n_dec,), PAST_LEN, dtype=jnp.int32))

    caches = (cache_k_w128, cache_v_w128, cache_k_w512, cache_v_w512, cache_k_w2048, cache_v_w2048)
    caches_t = [jnp.transpose(c, (0, 1, 3, 4, 2)) for c in caches]

    hp = x_prompt.reshape(rows, D_MODEL)
    hs = x_sample.reshape(n_dec, D_MODEL)
    ple_p = p_prompt.reshape(DEPTH, rows, PLE_DIM)
    ple_s = p_sample.reshape(DEPTH, n_dec, PLE_DIM)

    conv_p, conv_s = [], []
    states_p = None
    new_caches = None
    for i in range(DEPTH):
        res = _inproj(i, hp, cs_p, sn_p, g_in, w_in_b, conv_w, seq_len=seq_len, states=states_p)
        q, k, v, cbc, sg, cst = res[:6]
        states_p = res[6:]
        parts = []
        for g in range(N_GROUPS):
            parts += _attn_group(g, q, k, v, n_seq, seq_len)
        hp = _tail(i, hp, parts, cbc, sg, ple_p, tail_w, gains)
        conv_p.append(cst[:, V7X_SUBLANES - (CONV_WIDTH - 1):, :])

        prev = (state_conv[i, :, 0, :], state_conv[i, :, 1, :])
        qs, ks, vs, cbcs, sgs, us = _inproj(i, hs, cs_s, sn_s, g_in, w_in_b, conv_w, prev=prev)
        as_rows = lambda t: jnp.transpose(t, (1, 0, 2)).reshape(n_dec, 1, ATTN_WIDTH)
        attn_s, new_caches = _sample_attn(i, as_rows(qs), as_rows(ks), as_rows(vs), caches_t, new_caches)
        hs = _tail(i, hs, [attn_s.reshape(n_dec, GROUP_WIDTH)], cbcs, sgs, ple_s, tail_w, gains)
        conv_s.append(jnp.stack([prev[1], us], axis=1))

    to_rows = lambda c: jnp.transpose(c, (0, 1, 4, 2, 3))
    return (hp.reshape(n_seq, seq_len, D_MODEL), hs.reshape(n_dec, 1, D_MODEL),
            *[to_rows(c) for c in states_p], jnp.stack(conv_p),
            *[to_rows(c) for c in new_caches], jnp.stack(conv_s))
```

```python
import functools

import jax
import jax.numpy as jnp
from jax import lax
from jax.experimental import pallas as pl
from jax.experimental.pallas import tpu as pltpu

F32 = jnp.float32
BF16 = jnp.bfloat16

D_MODEL = 1024
DEPTH = 4
PAST_LEN = 8192
HEAD_DIM = 64
HEADS_PER_GROUP = 4
GROUPS = ((128, 1), (512, 4), (2048, 16))
N_GROUPS = len(GROUPS)
GROUP_WIDTH = HEADS_PER_GROUP * HEAD_DIM
ATTN_WIDTH = N_GROUPS * GROUP_WIDTH
CONV_CHANNELS = 768
CONV_WIDTH = 3
D_FF = 4 * D_MODEL
PLE_DIM = 256
ROPE_THETA = 10000.0
BLOCK = 128
RMS_EPS = 1e-6
NEG_INF = -1e30
IN_PROJ_WIDTH = 3 * ATTN_WIDTH + 3 * CONV_CHANNELS + 2 * D_MODEL
OFF_Q, OFF_K, OFF_V = 0, ATTN_WIDTH, 2 * ATTN_WIDTH
OFF_CB = 3 * ATTN_WIDTH
OFF_CC = OFF_CB + CONV_CHANNELS
OFF_CH = OFF_CC + CONV_CHANNELS
OFF_GA = OFF_CH + CONV_CHANNELS
OFF_GB = OFF_GA + D_MODEL

V7X_LANES = 128
V7X_SUBLANES = 8
V7X_VMEM_LIMIT = 60 * 1024 * 1024
SLABS = ATTN_WIDTH // V7X_LANES
SLABS_PER_GROUP = GROUP_WIDTH // V7X_LANES
HEADS_PER_SLAB = V7X_LANES // HEAD_DIM
ROW_TILE = 512
TAIL_ROW_TILE = 512
SPAN = BLOCK * GROUPS[-1][1]
SPAN_NO_DILATION = 1024
MLP_CHUNK = 1024
ROW_PARTS = 2
STRIDE_STEP = 4
ATTN_UNROLL = {1: 4, 4: 8, 16: 4}


def _rms(x, g):
    return x * lax.rsqrt(jnp.mean(x * x, axis=-1, keepdims=True) + RMS_EPS) * g


def _dot(a, b):
    return jnp.dot(a, b, preferred_element_type=F32)


def _resident(shape, index):
    return pl.BlockSpec(shape, index, pipeline_mode=pl.Buffered(1))


def _weight_spec(w, layer):
    if w.ndim == 2:
        return _resident(w.shape, lambda *_: (0, 0))
    return _resident((None,) + w.shape[1:], lambda *_: (layer, 0, 0))


def _inproj_kernel(sample, tm, tiles_per_seq, n_alias, *refs):
    if sample:
        (x_ref, cs_ref, sn_ref, g_ref, w_ref, cw_ref, p0_ref, p1_ref,
         q_ref, k_ref, v_ref, cbc_ref, sg_ref, u_ref) = refs
    else:
        x_ref, cs_ref, sn_ref, g_ref, w_ref, cw_ref = refs[:6]
        q_ref, k_ref, v_ref, cbc_ref, sg_ref, cst_ref = refs[6 + n_alias:12 + n_alias]
        state_refs = refs[12 + n_alias:12 + n_alias + 2 * N_GROUPS]
        u_scr = refs[-1]

    parts = 1 if sample else ROW_PARTS
    pr = tm // parts
    halo = V7X_SUBLANES
    lane = lax.broadcasted_iota(jnp.int32, (pr, V7X_LANES), 1)
    lower = (lane % HEAD_DIM) < (HEAD_DIM // 2)
    cw = cw_ref[...]

    if not sample:
        @pl.when(pl.program_id(0) % tiles_per_seq == 0)
        def _():
            u_scr[0:halo, :] = jnp.zeros((halo, CONV_CHANNELS), F32)

    xns, cbs = [], []
    for i in range(parts):
        rows = slice(i * pr, (i + 1) * pr)
        xn = _rms(x_ref[rows, :], g_ref[...]).astype(BF16)
        xns.append(xn)
        cs = cs_ref[rows, :]
        sn = sn_ref[rows, :]

        def proj(lo, width, xn=xn):
            return _dot(xn, w_ref[:, lo:lo + width])

        def rope_store(t, out_ref, scale, rows=rows, cs=cs, sn=sn):
            for c in range(SLABS):
                xc = t[:, c * V7X_LANES:(c + 1) * V7X_LANES]
                sw = jnp.where(lower, pltpu.roll(xc, V7X_LANES - HEAD_DIM // 2, 1),
                               pltpu.roll(xc, HEAD_DIM // 2, 1))
                r = xc * cs + sw * sn
                if scale is not None:
                    r = r * scale
                out_ref[c, rows, :] = r

        rope_store(proj(OFF_Q, ATTN_WIDTH), q_ref, HEAD_DIM ** -0.5)
        rope_store(proj(OFF_K, ATTN_WIDTH), k_ref, None)
        vv = proj(OFF_V, ATTN_WIDTH)
        for c in range(SLABS):
            v_ref[c, rows, :] = vv[:, c * V7X_LANES:(c + 1) * V7X_LANES]
        cbs.append(proj(OFF_CB, CONV_CHANNELS))
        u = proj(OFF_CC, CONV_CHANNELS) * proj(OFF_CH, CONV_CHANNELS)
        if sample:
            u_ref[...] = u
        else:
            u_scr[halo + i * pr:halo + (i + 1) * pr, :] = u

    for i in range(parts):
        rows = slice(i * pr, (i + 1) * pr)
        if sample:
            conv = cw[0:1] * p0_ref[...] + cw[1:2] * p1_ref[...] + cw[2:3] * u_ref[...]
        else:
            lo = halo + i * pr
            conv = (cw[0:1] * u_scr[lo - 2:lo + pr - 2, :] + cw[1:2] * u_scr[lo - 1:lo + pr - 1, :]
                    + cw[2:3] * u_scr[lo:lo + pr, :])
        cbc_ref[rows, :] = (cbs[i] * conv).astype(BF16)
        for lo, off in ((0, OFF_GA), (D_MODEL, OFF_GB)):
            gate = jax.nn.sigmoid(_dot(xns[i], w_ref[:, off:off + D_MODEL]))
            sg_ref[rows, lo:lo + D_MODEL] = gate.astype(BF16)

    if not sample:
        tail = u_scr[tm:tm + halo, :]
        u_scr[0:halo, :] = tail
        cst_ref[...] = tail
        for g, (window, _) in enumerate(GROUPS):
            keep = min(window, tm)
            for j, src in enumerate((k_ref, v_ref)):
                for half in range(SLABS_PER_GROUP):
                    last = src[g * SLABS_PER_GROUP + half, tm - keep:tm, :]
                    heads = slice(half * HEADS_PER_SLAB, (half + 1) * HEADS_PER_SLAB)
                    state_refs[2 * g + j][heads, :, :] = jnp.transpose(last).reshape(
                        HEADS_PER_SLAB, HEAD_DIM, keep)


def _inproj(layer, x, cs, sn, gains, w_in, conv_w, *, seq_len=None, prev=None, states=None):
    rows = x.shape[0]
    sample = prev is not None
    tm = rows if sample else ROW_TILE
    tiles_per_seq = 1 if sample else seq_len // tm
    grid = (rows // tm,)
    row = lambda i: (i, 0)
    slab_spec = pl.BlockSpec((SLABS, tm, V7X_LANES), lambda i: (0, i, 0))
    table_spec = pl.BlockSpec((tm, V7X_LANES), lambda i: (i % tiles_per_seq, 0))
    in_specs = [
        pl.BlockSpec((tm, D_MODEL), row),
        table_spec, table_spec,
        pl.BlockSpec((None, 1, D_MODEL), lambda i: (layer, 0, 0)),
        _weight_spec(w_in, layer),
        pl.BlockSpec((None, CONV_WIDTH, CONV_CHANNELS), lambda i: (layer, 0, 0)),
    ]
    args = [x, cs, sn, gains, w_in, conv_w]
    slab_shape = jax.ShapeDtypeStruct((SLABS, rows, V7X_LANES), F32)
    out_shape = [slab_shape, slab_shape, slab_shape,
                 jax.ShapeDtypeStruct((rows, CONV_CHANNELS), BF16),
                 jax.ShapeDtypeStruct((rows, 2 * D_MODEL), BF16)]
    out_specs = [slab_spec, slab_spec, slab_spec,
                 pl.BlockSpec((tm, CONV_CHANNELS), row),
                 pl.BlockSpec((tm, 2 * D_MODEL), row)]
    scratch = []
    aliases = {}
    n_alias = 0
    if sample:
        in_specs += [pl.BlockSpec((tm, CONV_CHANNELS), row)] * 2
        args += list(prev)
        out_shape.append(jax.ShapeDtypeStruct((rows, CONV_CHANNELS), F32))
        out_specs.append(pl.BlockSpec((tm, CONV_CHANNELS), row))
    else:
        n_seq = rows // seq_len
        out_shape.append(jax.ShapeDtypeStruct((n_seq, V7X_SUBLANES, CONV_CHANNELS), F32))
        out_specs.append(pl.BlockSpec((None, V7X_SUBLANES, CONV_CHANNELS),
                                      lambda i: (i // tiles_per_seq, 0, 0)))
        if states is not None:
            n_alias = len(states)
            in_specs += [pl.BlockSpec(memory_space=pl.ANY)] * n_alias
            aliases = {len(args) + j: len(out_shape) + j for j in range(n_alias)}
            args += list(states)
        for window, _ in GROUPS:
            keep = min(window, seq_len)
            blk = min(keep, tm)
            first_tile = tiles_per_seq - keep // blk

            def state_index(i, first_tile=first_tile):
                return (layer, i // tiles_per_seq, 0, 0,
                        jnp.maximum(i % tiles_per_seq - first_tile, 0))

            for _ in range(2):
                out_shape.append(jax.ShapeDtypeStruct(
                    (DEPTH, n_seq, HEADS_PER_GROUP, HEAD_DIM, keep), F32))
                out_specs.append(pl.BlockSpec((None, None, HEADS_PER_GROUP, HEAD_DIM, blk), state_index))
        scratch.append(pltpu.VMEM((tm + V7X_SUBLANES, CONV_CHANNELS), F32))
    return pl.pallas_call(
        functools.partial(_inproj_kernel, sample, tm, tiles_per_seq, n_alias),
        out_shape=out_shape, grid=grid, in_specs=in_specs, out_specs=out_specs,
        scratch_shapes=scratch, input_output_aliases=aliases,
        name="inproj_sample" if sample else "inproj_prompt",
        compiler_params=pltpu.CompilerParams(
            dimension_semantics=("arbitrary",), vmem_limit_bytes=V7X_VMEM_LIMIT),
    )(*args)


def _deinterleave(src, dst, tmp, dil, span):
    assert dil in (1, STRIDE_STEP, STRIDE_STEP * STRIDE_STEP)
    for c in range(SLABS_PER_GROUP):
        if dil == 1:
            dst[c] = src[c]
            continue
        seg = span // STRIDE_STEP
        stage = dst.at[c] if dil == STRIDE_STEP else tmp
        for r in range(STRIDE_STEP):
            stage[r * seg:(r + 1) * seg, :] = src[c, pl.ds(r, seg, stride=STRIDE_STEP), :]
        if dil == STRIDE_STEP:
            continue
        seg2 = seg // STRIDE_STEP
        for r in range(STRIDE_STEP):
            for r2 in range(STRIDE_STEP):
                res = r + STRIDE_STEP * r2
                dst[c, res * seg2:(res + 1) * seg2, :] = tmp[pl.ds(r * seg + r2, seg2, stride=STRIDE_STEP), :]


def _attn_kernel(dil, unroll, span, n_cast, n_other, *refs):
    q_ref, k_ref, v_ref = refs[:3]
    others = refs[3:3 + n_other]
    cast_in = refs[3 + n_other:3 + n_other + n_cast]
    n_in = 3 + n_other + n_cast
    n_res = 1 if n_other else 3
    res_refs = refs[n_in:n_in + n_res]
    cast_out = refs[n_in + n_res:n_in + n_res + n_cast]
    scratch = list(refs[n_in + n_res + n_cast:])
    kd, vd, bias = scratch[:3]
    qd = scratch[3] if dil > 1 else None
    tmp = scratch[4] if dil > STRIDE_STEP else None
    assert not (n_other and dil > 1)

    for src, dst in zip(cast_in, cast_out):
        dst[...] = src[...].astype(BF16)

    seg = span // dil
    span_idx = pl.program_id(1)
    par = span_idx % 2
    first_span = span_idx == 0

    qi = lax.broadcasted_iota(jnp.int32, (2 * BLOCK, 2 * BLOCK), 0) % BLOCK
    kj = lax.broadcasted_iota(jnp.int32, (2 * BLOCK, 2 * BLOCK), 1)
    own = kj >= BLOCK
    slack = jnp.where(own, qi - (kj - BLOCK), kj - qi)
    bias[0] = jnp.where(slack >= 0, 0.0, NEG_INF)
    bias[1] = jnp.where(jnp.where(own, slack, -1) >= 0, 0.0, NEG_INF)
    low_q = lax.broadcasted_iota(jnp.int32, (BLOCK, V7X_LANES), 1) < HEAD_DIM
    low_kv = lax.broadcasted_iota(jnp.int32, (2 * BLOCK, V7X_LANES), 1) < HEAD_DIM

    @pl.when(first_span)
    def _():
        kd[1] = jnp.zeros(kd.shape[1:], F32)
        vd[1] = jnp.zeros(vd.shape[1:], F32)

    q_rows = q_ref
    if dil > 1:
        _deinterleave(q_ref, qd, tmp, dil, span)
        q_rows = qd
    _deinterleave(k_ref, kd.at[par], tmp, dil, span)
    _deinterleave(v_ref, vd.at[par], tmp, dil, span)

    def body(blk, carry):
        res = blk % dil
        sub = blk // dil
        base = pl.multiple_of(res * seg + sub * BLOCK, BLOCK)
        inner = sub > 0
        pbuf = jnp.where(inner, par, 1 - par)
        prow = pl.multiple_of(jnp.where(inner, base - BLOCK, res * seg + seg - BLOCK), BLOCK)
        no_prev = jnp.logical_and(first_span, sub == 0).astype(jnp.int32)
        row0 = res + dil * BLOCK * sub
        out_rows = pl.ds(row0, BLOCK, stride=dil) if dil > 1 else pl.ds(base, BLOCK)
        for c in range(SLABS_PER_GROUP):
            q = q_rows[c, pl.ds(base, BLOCK), :]
            k2 = jnp.concatenate([kd[pbuf, c, pl.ds(prow, BLOCK), :],
                                  kd[par, c, pl.ds(base, BLOCK), :]], axis=0)
            v2 = jnp.concatenate([vd[pbuf, c, pl.ds(prow, BLOCK), :],
                                  vd[par, c, pl.ds(base, BLOCK), :]], axis=0)
            q2 = jnp.concatenate([jnp.where(low_q, q, 0.0), jnp.where(low_q, 0.0, q)], axis=0)
            s = lax.dot_general(q2.astype(BF16), k2.astype(BF16), (((1,), (1,)), ((), ())),
                                preferred_element_type=F32)
            s = s + bias[no_prev]
            m = jnp.max(s, axis=-1, keepdims=True)
            p = jnp.exp(s - m)
            l = jnp.sum(p, axis=-1, keepdims=True)
            pb = p.astype(BF16)
            p_cat = jnp.concatenate([pb[0:BLOCK], pb[BLOCK:2 * BLOCK]], axis=1)
            v_stack = jnp.concatenate([jnp.where(low_kv, v2, 0.0), jnp.where(low_kv, 0.0, v2)],
                                      axis=0).astype(BF16)
            o = _dot(p_cat, v_stack)
            l_slab = jnp.where(low_q, l[0:BLOCK], l[BLOCK:2 * BLOCK])
            m_slab = jnp.where(low_q, m[0:BLOCK], m[BLOCK:2 * BLOCK])
            if not n_other:
                o_ref, m_ref, l_ref = res_refs
                o_ref[c, out_rows, :] = o
                m_ref[c, out_rows, :] = m_slab
                l_ref[c, out_rows, :] = l_slab
                continue
            outs = [o] + [others[j][c, out_rows, :] for j in range(0, n_other, 3)]
            ms = [m_slab] + [others[j][c, out_rows, :] for j in range(1, n_other, 3)]
            ls = [l_slab] + [others[j][c, out_rows, :] for j in range(2, n_other, 3)]
            mx = functools.reduce(jnp.maximum, ms)
            es = [jnp.exp(mg - mx) for mg in ms]
            num = functools.reduce(lambda x, y: x + y, [e * og for e, og in zip(es, outs)])
            den = functools.reduce(lambda x, y: x + y, [e * lg for e, lg in zip(es, ls)])
            res_refs[0][out_rows, c * V7X_LANES:(c + 1) * V7X_LANES] = (num / den).astype(BF16)
        return carry

    lax.fori_loop(0, span // BLOCK, body, 0, unroll=unroll)


def _attn_group(group, q, k, v, n_seq, seq_len, casts, others=()):
    dil = GROUPS[group][1]
    rows = q.shape[1]
    span = SPAN if dil > 1 else SPAN_NO_DILATION
    spans_per_seq = seq_len // span
    steps = n_seq * spans_per_seq
    step = lambda b, n: b * spans_per_seq + n
    cast_in_specs, cast_out_specs, cast_shapes = [], [], []
    for w, layer in casts:
        blk = (w.shape[1] // steps, w.shape[2])
        cast_in_specs.append(pl.BlockSpec((None,) + blk, lambda b, n, layer=layer: (layer, step(b, n), 0)))
        cast_out_specs.append(pl.BlockSpec(blk, lambda b, n: (step(b, n), 0)))
        cast_shapes.append(jax.ShapeDtypeStruct(w.shape[1:], BF16))
    cur_spec = pl.BlockSpec((SLABS_PER_GROUP, span, V7X_LANES), lambda b, n: (group, step(b, n), 0))
    slab_spec = pl.BlockSpec((SLABS_PER_GROUP, span, V7X_LANES), lambda b, n: (0, step(b, n), 0))
    if others:
        res_shapes = [jax.ShapeDtypeStruct((rows, GROUP_WIDTH), BF16)]
        res_specs = [pl.BlockSpec((span, GROUP_WIDTH), lambda b, n: (step(b, n), 0))]
    else:
        res_shapes = [jax.ShapeDtypeStruct((SLABS_PER_GROUP, rows, V7X_LANES), F32)] * 3
        res_specs = [slab_spec] * 3
    span_buf = (SLABS_PER_GROUP, span, V7X_LANES)
    scratch = [pltpu.VMEM((2,) + span_buf, F32), pltpu.VMEM((2,) + span_buf, F32),
               pltpu.VMEM((2, 2 * BLOCK, 2 * BLOCK), F32)]
    if dil > 1:
        scratch.append(pltpu.VMEM(span_buf, F32))
    if dil > STRIDE_STEP:
        scratch.append(pltpu.VMEM((span, V7X_LANES), F32))
    return pl.pallas_call(
        functools.partial(_attn_kernel, dil, ATTN_UNROLL[dil], span, len(casts), len(others)),
        out_shape=res_shapes + cast_shapes, grid=(n_seq, spans_per_seq),
        in_specs=[cur_spec, cur_spec, cur_spec] + [slab_spec] * len(others) + cast_in_specs,
        out_specs=res_specs + cast_out_specs,
        scratch_shapes=scratch,
        name="attn_prompt_d%d" % dil,
        compiler_params=pltpu.CompilerParams(
            dimension_semantics=("arbitrary", "arbitrary"), vmem_limit_bytes=V7X_VMEM_LIMIT),
    )(q, k, v, *others, *[w for w, _ in casts])


def _tail_kernel(parts, *refs):
    (h_ref, attn_ref, cbc_ref, sg_ref, ple_ref, wa_ref, wc_ref, wo_ref, wu_ref, wd_ref, wg_ref, wp_ref,
     gains_ref, out_ref) = refs

    pr = h_ref.shape[0] // parts
    row_slices = [slice(i * pr, (i + 1) * pr) for i in range(parts)]

    mixed = []
    for rows in row_slices:
        a = _dot(attn_ref[rows, :].astype(BF16), wa_ref[...])
        c = _dot(cbc_ref[rows, :], wc_ref[...])
        sg = sg_ref[rows, :].astype(F32)
        mix = sg[:, 0:D_MODEL] * a + sg[:, D_MODEL:2 * D_MODEL] * c
        mixed.append(_dot(mix.astype(BF16), wo_ref[...]))

    embeds = [_dot(ple_ref[rows, :].astype(BF16), wp_ref[...]) for rows in row_slices]

    for rows, mo in zip(row_slices, mixed):
        out_ref[rows, :] = h_ref[rows, :] + _rms(mo, gains_ref[1:2, :])

    mlp = []
    for rows in row_slices:
        xn = _rms(out_ref[rows, :], gains_ref[2:3, :]).astype(BF16)
        acc = None
        for lo in range(0, D_FF, MLP_CHUNK):
            f = _dot(xn, wu_ref[:, lo:lo + MLP_CHUNK])
            f = jnp.square(jnp.maximum(f, 0.0)).astype(BF16)
            t = _dot(f, wd_ref[lo:lo + MLP_CHUNK, :])
            acc = t if acc is None else acc + t
        mlp.append(acc)

    for rows, acc in zip(row_slices, mlp):
        out_ref[rows, :] = out_ref[rows, :] + _rms(acc, gains_ref[3:4, :])

    for rows, embed in zip(row_slices, embeds):
        h = out_ref[rows, :]
        gate = jax.nn.sigmoid(_dot(_rms(h, gains_ref[4:5, :]).astype(BF16), wg_ref[...]))
        out_ref[rows, :] = h + gate * embed


def _tail(layer, h, attn, cbc, sg, ple, weights, gains, *, tm, parts):
    rows = h.shape[0]
    row = lambda i: (i, 0)
    in_specs = [
        pl.BlockSpec((tm, D_MODEL), row),
        pl.BlockSpec((tm, GROUP_WIDTH), row),
        pl.BlockSpec((tm, CONV_CHANNELS), row),
        pl.BlockSpec((tm, 2 * D_MODEL), row),
        pl.BlockSpec((None, tm, PLE_DIM), lambda i: (layer, i, 0)),
    ]
    in_specs += [_weight_spec(w, layer) for w in weights]
    in_specs += [pl.BlockSpec((None,) + gains.shape[1:], lambda i: (layer, 0, 0))]
    return pl.pallas_call(
        functools.partial(_tail_kernel, parts),
        out_shape=jax.ShapeDtypeStruct((rows, D_MODEL), F32),
        grid=(rows // tm,), in_specs=in_specs,
        out_specs=pl.BlockSpec((tm, D_MODEL), row),
        name="tail_prompt" if rows > tm else "tail_sample",
        compiler_params=pltpu.CompilerParams(
            dimension_semantics=("arbitrary",), vmem_limit_bytes=V7X_VMEM_LIMIT),
    )(h, attn, cbc, sg, ple, *weights, gains)


def _shift_in(old, new_col):
    width = old.shape[1]
    tiles = width // V7X_LANES
    lane = lax.broadcasted_iota(jnp.int32, (old.shape[0], V7X_LANES), 1)
    last = lane == V7X_LANES - 1
    rot = [pltpu.roll(old[:, t * V7X_LANES:(t + 1) * V7X_LANES], V7X_LANES - 1, 1)
           for t in range(tiles)]
    out = [jnp.where(last, rot[t + 1] if t + 1 < tiles else new_col, rot[t]) for t in range(tiles)]
    return out[0] if tiles == 1 else jnp.concatenate(out, axis=1)


def _sample_attn_kernel(n_alias, *refs):
    q_ref, kn_ref, vn_ref = refs[:3]
    caches = refs[3:3 + 2 * N_GROUPS]
    attn_ref = refs[3 + 2 * N_GROUPS + n_alias]
    outs = refs[4 + 2 * N_GROUPS + n_alias:]

    q = q_ref[...]
    kn = kn_ref[...]
    vn = vn_ref[...]
    head_row = lax.broadcasted_iota(jnp.int32, (V7X_SUBLANES, GROUP_WIDTH), 0)
    head_lane = lax.broadcasted_iota(jnp.int32, (V7X_SUBLANES, GROUP_WIDTH), 1) // HEAD_DIM
    own_head = head_row == head_lane

    parts = []
    for g, (window, dil) in enumerate(GROUPS):
        sl = slice(g * GROUP_WIDTH, (g + 1) * GROUP_WIDTH)
        kt = caches[2 * g][...].reshape(GROUP_WIDTH, window)
        vt = caches[2 * g + 1][...].reshape(GROUP_WIDTH, window)
        q_heads = jnp.where(own_head, jnp.broadcast_to(q[:, sl], own_head.shape), 0.0)
        s = _dot(q_heads.astype(BF16), kt.astype(BF16))
        pos = lax.broadcasted_iota(jnp.int32, s.shape, 1)
        s = jnp.where(pos % dil == 0, s, NEG_INF)
        s_new = jnp.sum(q_heads * kn[:, sl], axis=-1, keepdims=True)
        m = jnp.maximum(jnp.max(s, axis=-1, keepdims=True), s_new)
        p = jnp.exp(s - m)
        p_new = jnp.exp(s_new - m)
        l = jnp.sum(p, axis=-1, keepdims=True) + p_new
        o = lax.dot_general(p.astype(BF16), vt.astype(BF16), (((1,), (1,)), ((), ())),
                            preferred_element_type=F32) + p_new * vn[:, sl]
        parts.append((o / l, m, l))

        k_col = jnp.transpose(jnp.broadcast_to(kn[:, sl], (V7X_LANES, GROUP_WIDTH)))
        v_col = jnp.transpose(jnp.broadcast_to(vn[:, sl], (V7X_LANES, GROUP_WIDTH)))
        outs[2 * g][...] = _shift_in(kt, k_col).reshape(HEADS_PER_GROUP, HEAD_DIM, window)
        outs[2 * g + 1][...] = _shift_in(vt, v_col).reshape(HEADS_PER_GROUP, HEAD_DIM, window)

    mx = jnp.maximum(jnp.maximum(parts[0][1], parts[1][1]), parts[2][1])
    ws = [l * jnp.exp(m - mx) for _, m, l in parts]
    num = ws[0] * parts[0][0] + ws[1] * parts[1][0] + ws[2] * parts[2][0]
    merged = num / (ws[0] + ws[1] + ws[2])
    attn_ref[...] = jnp.sum(jnp.where(own_head, merged, 0.0), axis=0, keepdims=True)


def _sample_attn(layer, q, k_new, v_new, caches_t, new_caches):
    batch = q.shape[0]
    vec_spec = pl.BlockSpec((None, 1, ATTN_WIDTH), lambda b: (b, 0, 0))
    in_specs = [vec_spec, vec_spec, vec_spec]
    cache_specs = [pl.BlockSpec((None, None) + c.shape[2:], lambda b: (layer, b, 0, 0, 0))
                   for c in caches_t]
    in_specs += cache_specs
    args = [q, k_new, v_new, *caches_t]
    aliases = {}
    n_alias = 0
    if new_caches is not None:
        n_alias = len(new_caches)
        in_specs += [pl.BlockSpec(memory_space=pl.ANY)] * n_alias
        aliases = {len(args) + j: 1 + j for j in range(n_alias)}
        args += list(new_caches)
    out_shape = [jax.ShapeDtypeStruct((batch, 1, GROUP_WIDTH), F32)]
    out_shape += [jax.ShapeDtypeStruct(c.shape, F32) for c in caches_t]
    out_specs = [pl.BlockSpec((None, 1, GROUP_WIDTH), lambda b: (b, 0, 0))] + cache_specs
    res = pl.pallas_call(
        functools.partial(_sample_attn_kernel, n_alias),
        out_shape=out_shape, grid=(batch,), in_specs=in_specs, out_specs=out_specs,
        input_output_aliases=aliases, name="attn_sample",
        compiler_params=pltpu.CompilerParams(
            dimension_semantics=("arbitrary",), vmem_limit_bytes=V7X_VMEM_LIMIT),
    )(*args)
    return res[0], res[1:]


def _rope_tables(pos):
    half = HEAD_DIM // 2
    inv = jnp.power(ROPE_THETA, -2.0 * jnp.arange(half, dtype=F32) / HEAD_DIM)
    ang = pos.astype(F32)[:, None] * inv[None, :]
    cos, sin = jnp.cos(ang), jnp.sin(ang)
    reps = V7X_LANES // HEAD_DIM
    return (jnp.concatenate([cos, cos] * reps, axis=1), jnp.concatenate([-sin, sin] * reps, axis=1))


def kernel(x_prompt, x_sample, p_prompt, p_sample, cache_k_w128, cache_v_w128, cache_k_w512, cache_v_w512, cache_k_w2048, cache_v_w2048, state_conv, w_in, conv_w, w_attn_out, w_conv_out, w_o, g_pre_mix, g_post_mix, w_up, w_down, g_pre_mlp, g_post_mlp, g_ple, w_ple_gate, w_ple_proj):
    n_seq, seq_len, _ = x_prompt.shape
    n_dec = x_sample.shape[0]
    rows = n_seq * seq_len

    w_in_b = w_in[0].astype(BF16)
    w_attn_out_b = w_attn_out.astype(BF16)
    w_ple_proj_b = w_ple_proj.astype(BF16)
    gains = jnp.stack([g_pre_mix, g_post_mix, g_pre_mlp, g_post_mlp, g_ple], axis=1)
    g_in = g_pre_mix[:, None, :]

    cs_p, sn_p = _rope_tables(jnp.arange(seq_len, dtype=jnp.int32))
    cs_s, sn_s = _rope_tables(jnp.full((n_dec,), PAST_LEN, dtype=jnp.int32))

    caches = (cache_k_w128, cache_v_w128, cache_k_w512, cache_v_w512, cache_k_w2048, cache_v_w2048)
    caches_t = [jnp.transpose(c, (0, 1, 3, 4, 2)) for c in caches]

    hp = x_prompt.reshape(rows, D_MODEL)
    hs = x_sample.reshape(n_dec, D_MODEL)
    ple_p = p_prompt.reshape(DEPTH, rows, PLE_DIM)
    ple_s = p_sample.reshape(DEPTH, n_dec, PLE_DIM)

    conv_p, conv_s = [], []
    states_p = None
    new_caches = None
    for i in range(DEPTH):
        res = _inproj(i, hp, cs_p, sn_p, g_in, w_in_b, conv_w, seq_len=seq_len, states=states_p)
        q, k, v, cbc, sg, cst = res[:6]
        states_p = res[6:]
        casts = ([(w_up, i), (w_o, i), (w_ple_gate, i), (w_conv_out, i)], [(w_down, i)],
                 [(w_in, i + 1)] if i + 1 < DEPTH else [])
        res16 = _attn_group(2, q, k, v, n_seq, seq_len, casts[2])
        res4 = _attn_group(1, q, k, v, n_seq, seq_len, casts[1])
        res1 = _attn_group(0, q, k, v, n_seq, seq_len, casts[0], others=res4[:3] + res16[:3])
        attn_p, w_up_b, w_o_b, w_ple_gate_b, w_conv_out_b = res1
        w_down_b = res4[3]
        w_in_next = res16[3] if len(res16) > 3 else None
        tail_w = [w_attn_out_b, w_conv_out_b, w_o_b, w_up_b, w_down_b, w_ple_gate_b, w_ple_proj_b]
        hp = _tail(i, hp, attn_p, cbc, sg, ple_p, tail_w, gains, tm=TAIL_ROW_TILE, parts=ROW_PARTS)
        conv_p.append(cst[:, V7X_SUBLANES - (CONV_WIDTH - 1):, :])

        prev = (state_conv[i, :, 0, :], state_conv[i, :, 1, :])
        qs, ks, vs, cbcs, sgs, us = _inproj(i, hs, cs_s, sn_s, g_in, w_in_b, conv_w, prev=prev)
        as_rows = lambda t: jnp.transpose(t, (1, 0, 2)).reshape(n_dec, 1, ATTN_WIDTH)
        attn_s, new_caches = _sample_attn(i, as_rows(qs), as_rows(ks), as_rows(vs), caches_t, new_caches)
        hs = _tail(i, hs, attn_s.reshape(n_dec, GROUP_WIDTH), cbcs, sgs, ple_s, tail_w, gains,
                   tm=n_dec, parts=1)
        conv_s.append(jnp.stack([prev[1], us], axis=1))
        w_in_b = w_in_next

    to_rows = lambda c: jnp.transpose(c, (0, 1, 4, 2, 3))
    return (hp.reshape(n_seq, seq_len, D_MODEL), hs.reshape(n_dec, 1, D_MODEL),
            *[to_rows(c) for c in states_p], jnp.stack(conv_p),
            *[to_rows(c) for c in new_caches], jnp.stack(conv_s))
```

```python
import functools

import jax
import jax.numpy as jnp
from jax import lax
from jax.experimental import pallas as pl
from jax.experimental.pallas import tpu as pltpu

F32 = jnp.float32
BF16 = jnp.bfloat16

D_MODEL = 1024
DEPTH = 4
PAST_LEN = 8192
HEAD_DIM = 64
HEADS_PER_GROUP = 4
GROUPS = ((128, 1), (512, 4), (2048, 16))
N_GROUPS = len(GROUPS)
GROUP_WIDTH = HEADS_PER_GROUP * HEAD_DIM
ATTN_WIDTH = N_GROUPS * GROUP_WIDTH
CONV_CHANNELS = 768
CONV_WIDTH = 3
D_FF = 4 * D_MODEL
PLE_DIM = 256
ROPE_THETA = 10000.0
BLOCK = 128
RMS_EPS = 1e-6
NEG_INF = -1e30
IN_PROJ_WIDTH = 3 * ATTN_WIDTH + 3 * CONV_CHANNELS + 2 * D_MODEL
OFF_Q, OFF_K, OFF_V = 0, ATTN_WIDTH, 2 * ATTN_WIDTH
OFF_CB = 3 * ATTN_WIDTH
OFF_CC = OFF_CB + CONV_CHANNELS
OFF_CH = OFF_CC + CONV_CHANNELS
OFF_GA = OFF_CH + CONV_CHANNELS
OFF_GB = OFF_GA + D_MODEL

V7X_LANES = 128
V7X_SUBLANES = 8
V7X_VMEM_LIMIT = 60 * 1024 * 1024
SLABS = ATTN_WIDTH // V7X_LANES
SLABS_PER_GROUP = GROUP_WIDTH // V7X_LANES
HEADS_PER_SLAB = V7X_LANES // HEAD_DIM
ROW_TILE = 512
TAIL_ROW_TILE = 512
SPAN = BLOCK * GROUPS[-1][1]
SPAN_NO_DILATION = 1024
MLP_CHUNK = 1024
ROW_PARTS = 2
STRIDE_STEP = 4
ATTN_UNROLL = {1: 4, 4: 8, 16: 4}


def _rms(x, g):
    return x * lax.rsqrt(jnp.mean(x * x, axis=-1, keepdims=True) + RMS_EPS) * g


def _dot(a, b):
    return jnp.dot(a, b, preferred_element_type=F32)


def _resident(shape, index):
    return pl.BlockSpec(shape, index, pipeline_mode=pl.Buffered(1))


def _weight_spec(w, layer):
    if w.ndim == 2:
        return _resident(w.shape, lambda *_: (0, 0))
    return _resident((None,) + w.shape[1:], lambda *_: (layer, 0, 0))


N_KEY_SIDE_CACHES = 2 * N_GROUPS - 1
N_KEY_SIDE_IN = 3 + N_KEY_SIDE_CACHES
N_KEY_SIDE_OUT = N_KEY_SIDE_CACHES + 3


def _inproj_kernel(sample, tm, tiles_per_seq, n_alias, *refs):
    if sample:
        (x_ref, cs_ref, sn_ref, g_ref, w_ref, cw_ref, p0_ref, p1_ref,
         q_ref, k_ref, v_ref, cbc_ref, sg_ref, u_ref) = refs
    else:
        x_ref, cs_ref, sn_ref, g_ref, w_ref, cw_ref = refs[:6]
        side_in = refs[6:6 + N_KEY_SIDE_IN]
        n_in = 6 + N_KEY_SIDE_IN + n_alias
        q_ref, k_ref, v_ref, cbc_ref, sg_ref, cst_ref = refs[n_in:n_in + 6]
        state_refs = refs[n_in + 6:n_in + 6 + 2 * N_GROUPS]
        side_out = refs[n_in + 6 + 2 * N_GROUPS:n_in + 6 + 2 * N_GROUPS + N_KEY_SIDE_OUT]
        u_scr = refs[-1]

    parts = 1 if sample else ROW_PARTS
    pr = tm // parts
    halo = V7X_SUBLANES
    lane = lax.broadcasted_iota(jnp.int32, (pr, V7X_LANES), 1)
    lower = (lane % HEAD_DIM) < (HEAD_DIM // 2)
    cw = cw_ref[...]

    if not sample:
        @pl.when(pl.program_id(0) % tiles_per_seq == 0)
        def _():
            u_scr[0:halo, :] = jnp.zeros((halo, CONV_CHANNELS), F32)

        _sample_keys(side_in[0], side_in[1], side_in[2], side_in[3:], side_out[:N_KEY_SIDE_CACHES],
                     *side_out[N_KEY_SIDE_CACHES:])

    xns, cbs = [], []
    for i in range(parts):
        rows = slice(i * pr, (i + 1) * pr)
        xn = _rms(x_ref[rows, :], g_ref[...]).astype(BF16)
        xns.append(xn)
        cs = cs_ref[rows, :]
        sn = sn_ref[rows, :]

        def proj(lo, width, xn=xn):
            return _dot(xn, w_ref[:, lo:lo + width])

        def rope_store(t, out_ref, scale, rows=rows, cs=cs, sn=sn):
            for c in range(SLABS):
                xc = t[:, c * V7X_LANES:(c + 1) * V7X_LANES]
                sw = jnp.where(lower, pltpu.roll(xc, V7X_LANES - HEAD_DIM // 2, 1),
                               pltpu.roll(xc, HEAD_DIM // 2, 1))
                r = xc * cs + sw * sn
                if scale is not None:
                    r = r * scale
                out_ref[c, rows, :] = r

        rope_store(proj(OFF_Q, ATTN_WIDTH), q_ref, HEAD_DIM ** -0.5)
        rope_store(proj(OFF_K, ATTN_WIDTH), k_ref, None)
        vv = proj(OFF_V, ATTN_WIDTH)
        for c in range(SLABS):
            v_ref[c, rows, :] = vv[:, c * V7X_LANES:(c + 1) * V7X_LANES]
        cbs.append(proj(OFF_CB, CONV_CHANNELS))
        u = proj(OFF_CC, CONV_CHANNELS) * proj(OFF_CH, CONV_CHANNELS)
        if sample:
            u_ref[...] = u
        else:
            u_scr[halo + i * pr:halo + (i + 1) * pr, :] = u

    for i in range(parts):
        rows = slice(i * pr, (i + 1) * pr)
        if sample:
            conv = cw[0:1] * p0_ref[...] + cw[1:2] * p1_ref[...] + cw[2:3] * u_ref[...]
        else:
            lo = halo + i * pr
            conv = (cw[0:1] * u_scr[lo - 2:lo + pr - 2, :] + cw[1:2] * u_scr[lo - 1:lo + pr - 1, :]
                    + cw[2:3] * u_scr[lo:lo + pr, :])
        cbc_ref[rows, :] = (cbs[i] * conv).astype(BF16)
        for lo, off in ((0, OFF_GA), (D_MODEL, OFF_GB)):
            gate = jax.nn.sigmoid(_dot(xns[i], w_ref[:, off:off + D_MODEL]))
            sg_ref[rows, lo:lo + D_MODEL] = gate.astype(BF16)

    if not sample:
        tail = u_scr[tm:tm + halo, :]
        u_scr[0:halo, :] = tail
        cst_ref[...] = tail
        for g, (window, _) in enumerate(GROUPS):
            keep = min(window, tm)
            for j, src in enumerate((k_ref, v_ref)):
                for half in range(SLABS_PER_GROUP):
                    last = src[g * SLABS_PER_GROUP + half, tm - keep:tm, :]
                    heads = slice(half * HEADS_PER_SLAB, (half + 1) * HEADS_PER_SLAB)
                    state_refs[2 * g + j][heads, :, :] = jnp.transpose(last).reshape(
                        HEADS_PER_SLAB, HEAD_DIM, keep)


def _inproj(layer, x, cs, sn, gains, w_in, conv_w, *, seq_len=None, prev=None, states=None, side=None):
    rows = x.shape[0]
    sample = prev is not None
    tm = rows if sample else ROW_TILE
    tiles_per_seq = 1 if sample else seq_len // tm
    grid = (rows // tm,)
    row = lambda i: (i, 0)
    slab_spec = pl.BlockSpec((SLABS, tm, V7X_LANES), lambda i: (0, i, 0))
    table_spec = pl.BlockSpec((tm, V7X_LANES), lambda i: (i % tiles_per_seq, 0))
    in_specs = [
        pl.BlockSpec((tm, D_MODEL), row),
        table_spec, table_spec,
        pl.BlockSpec((None, 1, D_MODEL), lambda i: (layer, 0, 0)),
        _weight_spec(w_in, layer),
        pl.BlockSpec((None, CONV_WIDTH, CONV_CHANNELS), lambda i: (layer, 0, 0)),
    ]
    args = [x, cs, sn, gains, w_in, conv_w]
    slab_shape = jax.ShapeDtypeStruct((SLABS, rows, V7X_LANES), F32)
    out_shape = [slab_shape, slab_shape, slab_shape,
                 jax.ShapeDtypeStruct((rows, CONV_CHANNELS), BF16),
                 jax.ShapeDtypeStruct((rows, 2 * D_MODEL), BF16)]
    out_specs = [slab_spec, slab_spec, slab_spec,
                 pl.BlockSpec((tm, CONV_CHANNELS), row),
                 pl.BlockSpec((tm, 2 * D_MODEL), row)]
    scratch = []
    aliases = {}
    n_alias = 0
    if sample:
        in_specs += [pl.BlockSpec((tm, CONV_CHANNELS), row)] * 2
        args += list(prev)
        out_shape.append(jax.ShapeDtypeStruct((rows, CONV_CHANNELS), F32))
        out_specs.append(pl.BlockSpec((tm, CONV_CHANNELS), row))
    else:
        n_seq = rows // seq_len
        out_shape.append(jax.ShapeDtypeStruct((n_seq, V7X_SUBLANES, CONV_CHANNELS), F32))
        out_specs.append(pl.BlockSpec((None, V7X_SUBLANES, CONV_CHANNELS),
                                      lambda i: (i // tiles_per_seq, 0, 0)))
        sq, sk, sv, side_caches, side_new = side
        n_dec = sq.shape[0]
        assert n_dec == grid[0] and len(side_caches) == N_KEY_SIDE_CACHES
        vec_spec = pl.BlockSpec((None, 1, ATTN_WIDTH), lambda i: (i, 0, 0))
        cache_specs = [pl.BlockSpec((None, None) + c.shape[2:], lambda i: (layer, i, 0, 0, 0))
                       for c in side_caches]
        in_specs += [vec_spec] * 3 + cache_specs
        args += [sq, sk, sv, *side_caches]
        if states is not None:
            carried = list(states) + list(side_new)
            n_alias = len(carried)
            in_specs += [pl.BlockSpec(memory_space=pl.ANY)] * n_alias
            aliases = {len(args) + j: len(out_shape) + j for j in range(n_alias)}
            args += carried
        for window, _ in GROUPS:
            keep = min(window, seq_len)
            blk = min(keep, tm)
            first_tile = tiles_per_seq - keep // blk

            def state_index(i, first_tile=first_tile):
                return (layer, i // tiles_per_seq, 0, 0,
                        jnp.maximum(i % tiles_per_seq - first_tile, 0))

            for _ in range(2):
                out_shape.append(jax.ShapeDtypeStruct(
                    (DEPTH, n_seq, HEADS_PER_GROUP, HEAD_DIM, keep), F32))
                out_specs.append(pl.BlockSpec((None, None, HEADS_PER_GROUP, HEAD_DIM, blk), state_index))
        out_shape += [jax.ShapeDtypeStruct(c.shape, F32) for c in side_caches]
        out_specs += cache_specs
        widest = GROUPS[-1][0]
        for shape in ((N_GROUPS - 1, V7X_SUBLANES, GROUP_WIDTH), (V7X_SUBLANES, widest),
                      (V7X_SUBLANES, V7X_LANES)):
            out_shape.append(jax.ShapeDtypeStruct((n_dec,) + shape, F32))
            out_specs.append(pl.BlockSpec((None,) + shape, lambda i, nd=len(shape): (i,) + (0,) * nd))
        scratch.append(pltpu.VMEM((tm + V7X_SUBLANES, CONV_CHANNELS), F32))
    return pl.pallas_call(
        functools.partial(_inproj_kernel, sample, tm, tiles_per_seq, n_alias),
        out_shape=out_shape, grid=grid, in_specs=in_specs, out_specs=out_specs,
        scratch_shapes=scratch, input_output_aliases=aliases,
        name="inproj_sample" if sample else "inproj_prompt",
        compiler_params=pltpu.CompilerParams(
            dimension_semantics=("arbitrary",), vmem_limit_bytes=V7X_VMEM_LIMIT),
    )(*args)


def _deinterleave(src, dst, tmp, dil, span):
    assert dil in (1, STRIDE_STEP, STRIDE_STEP * STRIDE_STEP)
    for c in range(SLABS_PER_GROUP):
        if dil == 1:
            dst[c] = src[c]
            continue
        seg = span // STRIDE_STEP
        stage = dst.at[c] if dil == STRIDE_STEP else tmp
        for r in range(STRIDE_STEP):
            stage[r * seg:(r + 1) * seg, :] = src[c, pl.ds(r, seg, stride=STRIDE_STEP), :]
        if dil == STRIDE_STEP:
            continue
        seg2 = seg // STRIDE_STEP
        for r in range(STRIDE_STEP):
            for r2 in range(STRIDE_STEP):
                res = r + STRIDE_STEP * r2
                dst[c, res * seg2:(res + 1) * seg2, :] = tmp[pl.ds(r * seg + r2, seg2, stride=STRIDE_STEP), :]


def _attn_kernel(dil, unroll, span, n_cast, n_other, *refs):
    q_ref, k_ref, v_ref = refs[:3]
    others = refs[3:3 + n_other]
    cast_in = refs[3 + n_other:3 + n_other + n_cast]
    n_in = 3 + n_other + n_cast
    n_res = 1 if n_other else 3
    res_refs = refs[n_in:n_in + n_res]
    cast_out = refs[n_in + n_res:n_in + n_res + n_cast]
    scratch = list(refs[n_in + n_res + n_cast:])
    kd, vd, bias = scratch[:3]
    qd = scratch[3] if dil > 1 else None
    tmp = scratch[4] if dil > STRIDE_STEP else None
    assert not (n_other and dil > 1)

    for src, dst in zip(cast_in, cast_out):
        dst[...] = src[...].astype(BF16)

    seg = span // dil
    span_idx = pl.program_id(1)
    par = span_idx % 2
    first_span = span_idx == 0

    qi = lax.broadcasted_iota(jnp.int32, (2 * BLOCK, 2 * BLOCK), 0) % BLOCK
    kj = lax.broadcasted_iota(jnp.int32, (2 * BLOCK, 2 * BLOCK), 1)
    own = kj >= BLOCK
    slack = jnp.where(own, qi - (kj - BLOCK), kj - qi)
    bias[0] = jnp.where(slack >= 0, 0.0, NEG_INF)
    bias[1] = jnp.where(jnp.where(own, slack, -1) >= 0, 0.0, NEG_INF)
    low_q = lax.broadcasted_iota(jnp.int32, (BLOCK, V7X_LANES), 1) < HEAD_DIM
    low_kv = lax.broadcasted_iota(jnp.int32, (2 * BLOCK, V7X_LANES), 1) < HEAD_DIM

    @pl.when(first_span)
    def _():
        kd[1] = jnp.zeros(kd.shape[1:], F32)
        vd[1] = jnp.zeros(vd.shape[1:], F32)

    q_rows = q_ref
    if dil > 1:
        _deinterleave(q_ref, qd, tmp, dil, span)
        q_rows = qd
    _deinterleave(k_ref, kd.at[par], tmp, dil, span)
    _deinterleave(v_ref, vd.at[par], tmp, dil, span)

    def body(blk, carry):
        res = blk % dil
        sub = blk // dil
        base = pl.multiple_of(res * seg + sub * BLOCK, BLOCK)
        inner = sub > 0
        pbuf = jnp.where(inner, par, 1 - par)
        prow = pl.multiple_of(jnp.where(inner, base - BLOCK, res * seg + seg - BLOCK), BLOCK)
        no_prev = jnp.logical_and(first_span, sub == 0).astype(jnp.int32)
        row0 = res + dil * BLOCK * sub
        out_rows = pl.ds(row0, BLOCK, stride=dil) if dil > 1 else pl.ds(base, BLOCK)
        for c in range(SLABS_PER_GROUP):
            q = q_rows[c, pl.ds(base, BLOCK), :]
            k2 = jnp.concatenate([kd[pbuf, c, pl.ds(prow, BLOCK), :],
                                  kd[par, c, pl.ds(base, BLOCK), :]], axis=0)
            v2 = jnp.concatenate([vd[pbuf, c, pl.ds(prow, BLOCK), :],
                                  vd[par, c, pl.ds(base, BLOCK), :]], axis=0)
            q2 = jnp.concatenate([jnp.where(low_q, q, 0.0), jnp.where(low_q, 0.0, q)], axis=0)
            s = lax.dot_general(q2.astype(BF16), k2.astype(BF16), (((1,), (1,)), ((), ())),
                                preferred_element_type=F32)
            s = s + bias[no_prev]
            m = jnp.max(s, axis=-1, keepdims=True)
            p = jnp.exp(s - m)
            l = jnp.sum(p, axis=-1, keepdims=True)
            pb = p.astype(BF16)
            p_cat = jnp.concatenate([pb[0:BLOCK], pb[BLOCK:2 * BLOCK]], axis=1)
            v_stack = jnp.concatenate([jnp.where(low_kv, v2, 0.0), jnp.where(low_kv, 0.0, v2)],
                                      axis=0).astype(BF16)
            o = _dot(p_cat, v_stack)
            l_slab = jnp.where(low_q, l[0:BLOCK], l[BLOCK:2 * BLOCK])
            m_slab = jnp.where(low_q, m[0:BLOCK], m[BLOCK:2 * BLOCK])
            if not n_other:
                o_ref, m_ref, l_ref = res_refs
                o_ref[c, out_rows, :] = o
                m_ref[c, out_rows, :] = m_slab
                l_ref[c, out_rows, :] = l_slab
                continue
            outs = [o] + [others[j][c, out_rows, :] for j in range(0, n_other, 3)]
            ms = [m_slab] + [others[j][c, out_rows, :] for j in range(1, n_other, 3)]
            ls = [l_slab] + [others[j][c, out_rows, :] for j in range(2, n_other, 3)]
            mx = functools.reduce(jnp.maximum, ms)
            es = [jnp.exp(mg - mx) for mg in ms]
            num = functools.reduce(lambda x, y: x + y, [e * og for e, og in zip(es, outs)])
            den = functools.reduce(lambda x, y: x + y, [e * lg for e, lg in zip(es, ls)])
            res_refs[0][out_rows, c * V7X_LANES:(c + 1) * V7X_LANES] = (num / den).astype(BF16)
        return carry

    lax.fori_loop(0, span // BLOCK, body, 0, unroll=unroll)


def _attn_group(group, q, k, v, n_seq, seq_len, casts, others=()):
    dil = GROUPS[group][1]
    rows = q.shape[1]
    span = SPAN if dil > 1 else SPAN_NO_DILATION
    spans_per_seq = seq_len // span
    steps = n_seq * spans_per_seq
    step = lambda b, n: b * spans_per_seq + n
    cast_in_specs, cast_out_specs, cast_shapes = [], [], []
    for w, layer in casts:
        blk = (w.shape[1] // steps, w.shape[2])
        cast_in_specs.append(pl.BlockSpec((None,) + blk, lambda b, n, layer=layer: (layer, step(b, n), 0)))
        cast_out_specs.append(pl.BlockSpec(blk, lambda b, n: (step(b, n), 0)))
        cast_shapes.append(jax.ShapeDtypeStruct(w.shape[1:], BF16))
    cur_spec = pl.BlockSpec((SLABS_PER_GROUP, span, V7X_LANES), lambda b, n: (group, step(b, n), 0))
    slab_spec = pl.BlockSpec((SLABS_PER_GROUP, span, V7X_LANES), lambda b, n: (0, step(b, n), 0))
    if others:
        res_shapes = [jax.ShapeDtypeStruct((rows, GROUP_WIDTH), BF16)]
        res_specs = [pl.BlockSpec((span, GROUP_WIDTH), lambda b, n: (step(b, n), 0))]
    else:
        res_shapes = [jax.ShapeDtypeStruct((SLABS_PER_GROUP, rows, V7X_LANES), F32)] * 3
        res_specs = [slab_spec] * 3
    span_buf = (SLABS_PER_GROUP, span, V7X_LANES)
    scratch = [pltpu.VMEM((2,) + span_buf, F32), pltpu.VMEM((2,) + span_buf, F32),
               pltpu.VMEM((2, 2 * BLOCK, 2 * BLOCK), F32)]
    if dil > 1:
        scratch.append(pltpu.VMEM(span_buf, F32))
    if dil > STRIDE_STEP:
        scratch.append(pltpu.VMEM((span, V7X_LANES), F32))
    return pl.pallas_call(
        functools.partial(_attn_kernel, dil, ATTN_UNROLL[dil], span, len(casts), len(others)),
        out_shape=res_shapes + cast_shapes, grid=(n_seq, spans_per_seq),
        in_specs=[cur_spec, cur_spec, cur_spec] + [slab_spec] * len(others) + cast_in_specs,
        out_specs=res_specs + cast_out_specs,
        scratch_shapes=scratch,
        name="attn_prompt_d%d" % dil,
        compiler_params=pltpu.CompilerParams(
            dimension_semantics=("arbitrary", "arbitrary"), vmem_limit_bytes=V7X_VMEM_LIMIT),
    )(q, k, v, *others, *[w for w, _ in casts])


def _tail_kernel(parts, *refs):
    (h_ref, attn_ref, cbc_ref, sg_ref, ple_ref, wa_ref, wc_ref, wo_ref, wu_ref, wd_ref, wg_ref, wp_ref,
     gains_ref, out_ref) = refs

    pr = h_ref.shape[0] // parts
    row_slices = [slice(i * pr, (i + 1) * pr) for i in range(parts)]

    mixed = []
    for rows in row_slices:
        a = _dot(attn_ref[rows, :].astype(BF16), wa_ref[...])
        c = _dot(cbc_ref[rows, :], wc_ref[...])
        sg = sg_ref[rows, :].astype(F32)
        mix = sg[:, 0:D_MODEL] * a + sg[:, D_MODEL:2 * D_MODEL] * c
        mixed.append(_dot(mix.astype(BF16), wo_ref[...]))

    embeds = [_dot(ple_ref[rows, :].astype(BF16), wp_ref[...]) for rows in row_slices]

    for rows, mo in zip(row_slices, mixed):
        out_ref[rows, :] = h_ref[rows, :] + _rms(mo, gains_ref[1:2, :])

    mlp = []
    for rows in row_slices:
        xn = _rms(out_ref[rows, :], gains_ref[2:3, :]).astype(BF16)
        acc = None
        for lo in range(0, D_FF, MLP_CHUNK):
            f = _dot(xn, wu_ref[:, lo:lo + MLP_CHUNK])
            f = jnp.square(jnp.maximum(f, 0.0)).astype(BF16)
            t = _dot(f, wd_ref[lo:lo + MLP_CHUNK, :])
            acc = t if acc is None else acc + t
        mlp.append(acc)

    for rows, acc in zip(row_slices, mlp):
        out_ref[rows, :] = out_ref[rows, :] + _rms(acc, gains_ref[3:4, :])

    for rows, embed in zip(row_slices, embeds):
        h = out_ref[rows, :]
        gate = jax.nn.sigmoid(_dot(_rms(h, gains_ref[4:5, :]).astype(BF16), wg_ref[...]))
        out_ref[rows, :] = h + gate * embed


def _tail(layer, h, attn, cbc, sg, ple, weights, gains, *, tm, parts):
    rows = h.shape[0]
    row = lambda i: (i, 0)
    in_specs = [
        pl.BlockSpec((tm, D_MODEL), row),
        pl.BlockSpec((tm, GROUP_WIDTH), row),
        pl.BlockSpec((tm, CONV_CHANNELS), row),
        pl.BlockSpec((tm, 2 * D_MODEL), row),
        pl.BlockSpec((None, tm, PLE_DIM), lambda i: (layer, i, 0)),
    ]
    in_specs += [_weight_spec(w, layer) for w in weights]
    in_specs += [pl.BlockSpec((None,) + gains.shape[1:], lambda i: (layer, 0, 0))]
    return pl.pallas_call(
        functools.partial(_tail_kernel, parts),
        out_shape=jax.ShapeDtypeStruct((rows, D_MODEL), F32),
        grid=(rows // tm,), in_specs=in_specs,
        out_specs=pl.BlockSpec((tm, D_MODEL), row),
        name="tail_prompt" if rows > tm else "tail_sample",
        compiler_params=pltpu.CompilerParams(
            dimension_semantics=("arbitrary",), vmem_limit_bytes=V7X_VMEM_LIMIT),
    )(h, attn, cbc, sg, ple, *weights, gains)


def _shift_in(old, new_col):
    width = old.shape[1]
    tiles = width // V7X_LANES
    lane = lax.broadcasted_iota(jnp.int32, (old.shape[0], V7X_LANES), 1)
    last = lane == V7X_LANES - 1
    rot = [pltpu.roll(old[:, t * V7X_LANES:(t + 1) * V7X_LANES], V7X_LANES - 1, 1)
           for t in range(tiles)]
    out = [jnp.where(last, rot[t + 1] if t + 1 < tiles else new_col, rot[t]) for t in range(tiles)]
    return out[0] if tiles == 1 else jnp.concatenate(out, axis=1)


def _own_head_mask():
    head_row = lax.broadcasted_iota(jnp.int32, (V7X_SUBLANES, GROUP_WIDTH), 0)
    head_lane = lax.broadcasted_iota(jnp.int32, (V7X_SUBLANES, GROUP_WIDTH), 1) // HEAD_DIM
    return head_row == head_lane


def _new_column(row):
    return jnp.transpose(jnp.broadcast_to(row, (V7X_LANES, GROUP_WIDTH)))


def _values_out(p, p_new, vt, v_new):
    return lax.dot_general(p.astype(BF16), vt.astype(BF16), (((1,), (1,)), ((), ())),
                           preferred_element_type=F32) + p_new * v_new


def _sample_keys(q_ref, kn_ref, vn_ref, caches, outs, part_o_ref, part_p_ref, part_s_ref):
    q = q_ref[...]
    kn = kn_ref[...]
    vn = vn_ref[...]
    own_head = _own_head_mask()
    stats = []
    for g, (window, dil) in enumerate(GROUPS):
        sl = slice(g * GROUP_WIDTH, (g + 1) * GROUP_WIDTH)
        kt = caches[2 * g][...].reshape(GROUP_WIDTH, window)
        q_heads = jnp.where(own_head, jnp.broadcast_to(q[:, sl], own_head.shape), 0.0)
        s = _dot(q_heads.astype(BF16), kt.astype(BF16))
        pos = lax.broadcasted_iota(jnp.int32, s.shape, 1)
        s = jnp.where(pos % dil == 0, s, NEG_INF)
        s_new = jnp.sum(q_heads * kn[:, sl], axis=-1, keepdims=True)
        m = jnp.maximum(jnp.max(s, axis=-1, keepdims=True), s_new)
        p = jnp.exp(s - m)
        p_new = jnp.exp(s_new - m)
        l = jnp.sum(p, axis=-1, keepdims=True) + p_new
        stats += [m, l]
        outs[2 * g][...] = _shift_in(kt, _new_column(kn[:, sl])).reshape(HEADS_PER_GROUP, HEAD_DIM, window)
        if g + 1 < N_GROUPS:
            vt = caches[2 * g + 1][...].reshape(GROUP_WIDTH, window)
            part_o_ref[g] = _values_out(p, p_new, vt, vn[:, sl]) / l
            outs[2 * g + 1][...] = _shift_in(vt, _new_column(vn[:, sl])).reshape(
                HEADS_PER_GROUP, HEAD_DIM, window)
        else:
            part_p_ref[...] = p
            stats.append(p_new)
    lane = lax.broadcasted_iota(jnp.int32, (V7X_SUBLANES, V7X_LANES), 1)
    packed = jnp.zeros((V7X_SUBLANES, V7X_LANES), F32)
    for j, col in enumerate(stats):
        packed = jnp.where(lane == j, col, packed)
    part_s_ref[...] = packed


def _sample_values_kernel(n_alias, vn_ref, part_o_ref, part_p_ref, part_s_ref, cv_ref, *rest):
    attn_ref, nv_ref = rest[n_alias], rest[n_alias + 1]
    g = N_GROUPS - 1
    window = GROUPS[g][0]
    stats = part_s_ref[...]
    col = lambda j: stats[:, j:j + 1]
    v_new = vn_ref[...][:, g * GROUP_WIDTH:(g + 1) * GROUP_WIDTH]
    vt = cv_ref[...].reshape(GROUP_WIDTH, window)
    parts = [(part_o_ref[j], col(2 * j), col(2 * j + 1)) for j in range(g)]
    parts.append((_values_out(part_p_ref[...], col(2 * N_GROUPS), vt, v_new) / col(2 * g + 1),
                  col(2 * g), col(2 * g + 1)))
    mx = jnp.maximum(jnp.maximum(parts[0][1], parts[1][1]), parts[2][1])
    ws = [l * jnp.exp(m - mx) for _, m, l in parts]
    num = ws[0] * parts[0][0] + ws[1] * parts[1][0] + ws[2] * parts[2][0]
    merged = num / (ws[0] + ws[1] + ws[2])
    attn_ref[...] = jnp.sum(jnp.where(_own_head_mask(), merged, 0.0), axis=0, keepdims=True)
    nv_ref[...] = _shift_in(vt, _new_column(v_new)).reshape(HEADS_PER_GROUP, HEAD_DIM, window)


def _sample_values(layer, v_new, part_o, part_p, part_s, cache_v, new_cache_v):
    batch = v_new.shape[0]
    per_seq = lambda a: pl.BlockSpec((None,) + a.shape[1:], lambda b: (b,) + (0,) * (a.ndim - 1))
    cache_spec = pl.BlockSpec((None, None) + cache_v.shape[2:], lambda b: (layer, b, 0, 0, 0))
    in_specs = [per_seq(v_new), per_seq(part_o), per_seq(part_p), per_seq(part_s), cache_spec]
    args = [v_new, part_o, part_p, part_s, cache_v]
    aliases = {}
    n_alias = 0
    if new_cache_v is not None:
        n_alias = 1
        in_specs.append(pl.BlockSpec(memory_space=pl.ANY))
        aliases = {len(args): 1}
        args.append(new_cache_v)
    return pl.pallas_call(
        functools.partial(_sample_values_kernel, n_alias),
        out_shape=[jax.ShapeDtypeStruct((batch, 1, GROUP_WIDTH), F32),
                   jax.ShapeDtypeStruct(cache_v.shape, F32)],
        grid=(batch,), in_specs=in_specs,
        out_specs=[pl.BlockSpec((None, 1, GROUP_WIDTH), lambda b: (b, 0, 0)), cache_spec],
        input_output_aliases=aliases, name="attn_sample_values",
        compiler_params=pltpu.CompilerParams(
            dimension_semantics=("arbitrary",), vmem_limit_bytes=V7X_VMEM_LIMIT),
    )(*args)


def _rope_tables(pos):
    half = HEAD_DIM // 2
    inv = jnp.power(ROPE_THETA, -2.0 * jnp.arange(half, dtype=F32) / HEAD_DIM)
    ang = pos.astype(F32)[:, None] * inv[None, :]
    cos, sin = jnp.cos(ang), jnp.sin(ang)
    reps = V7X_LANES // HEAD_DIM
    return (jnp.concatenate([cos, cos] * reps, axis=1), jnp.concatenate([-sin, sin] * reps, axis=1))


def kernel(x_prompt, x_sample, p_prompt, p_sample, cache_k_w128, cache_v_w128, cache_k_w512, cache_v_w512, cache_k_w2048, cache_v_w2048, state_conv, w_in, conv_w, w_attn_out, w_conv_out, w_o, g_pre_mix, g_post_mix, w_up, w_down, g_pre_mlp, g_post_mlp, g_ple, w_ple_gate, w_ple_proj):
    n_seq, seq_len, _ = x_prompt.shape
    n_dec = x_sample.shape[0]
    rows = n_seq * seq_len

    w_in_b = w_in[0].astype(BF16)
    w_attn_out_b = w_attn_out.astype(BF16)
    w_ple_proj_b = w_ple_proj.astype(BF16)
    gains = jnp.stack([g_pre_mix, g_post_mix, g_pre_mlp, g_post_mlp, g_ple], axis=1)
    g_in = g_pre_mix[:, None, :]

    cs_p, sn_p = _rope_tables(jnp.arange(seq_len, dtype=jnp.int32))
    cs_s, sn_s = _rope_tables(jnp.full((n_dec,), PAST_LEN, dtype=jnp.int32))

    caches = (cache_k_w128, cache_v_w128, cache_k_w512, cache_v_w512, cache_k_w2048, cache_v_w2048)
    caches_t = [jnp.transpose(c, (0, 1, 3, 4, 2)) for c in caches]

    hp = x_prompt.reshape(rows, D_MODEL)
    hs = x_sample.reshape(n_dec, D_MODEL)
    ple_p = p_prompt.reshape(DEPTH, rows, PLE_DIM)
    ple_s = p_sample.reshape(DEPTH, n_dec, PLE_DIM)

    conv_p, conv_s = [], []
    states_p = None
    key_caches = None
    value_cache = None
    n_states = 2 * N_GROUPS
    as_rows = lambda t: jnp.transpose(t, (1, 0, 2)).reshape(n_dec, 1, ATTN_WIDTH)
    for i in range(DEPTH):
        prev = (state_conv[i, :, 0, :], state_conv[i, :, 1, :])
        qs, ks, vs, cbcs, sgs, us = _inproj(i, hs, cs_s, sn_s, g_in, w_in_b, conv_w, prev=prev)
        vs_rows = as_rows(vs)
        side = (as_rows(qs), as_rows(ks), vs_rows, caches_t[:N_KEY_SIDE_CACHES], key_caches)
        res = _inproj(i, hp, cs_p, sn_p, g_in, w_in_b, conv_w, seq_len=seq_len, states=states_p,
                      side=side)
        q, k, v, cbc, sg, cst = res[:6]
        states_p = res[6:6 + n_states]
        key_caches = res[6 + n_states:6 + n_states + N_KEY_SIDE_CACHES]
        part_o, part_p, part_s = res[6 + n_states + N_KEY_SIDE_CACHES:]
        casts = ([(w_up, i), (w_o, i), (w_ple_gate, i), (w_conv_out, i)], [(w_down, i)],
                 [(w_in, i + 1)] if i + 1 < DEPTH else [])
        res16 = _attn_group(2, q, k, v, n_seq, seq_len, casts[2])
        res4 = _attn_group(1, q, k, v, n_seq, seq_len, casts[1])
        res1 = _attn_group(0, q, k, v, n_seq, seq_len, casts[0], others=res4[:3] + res16[:3])
        attn_p, w_up_b, w_o_b, w_ple_gate_b, w_conv_out_b = res1
        w_down_b = res4[3]
        w_in_next = res16[3] if len(res16) > 3 else None
        tail_w = [w_attn_out_b, w_conv_out_b, w_o_b, w_up_b, w_down_b, w_ple_gate_b, w_ple_proj_b]
        hp = _tail(i, hp, attn_p, cbc, sg, ple_p, tail_w, gains, tm=TAIL_ROW_TILE, parts=ROW_PARTS)
        conv_p.append(cst[:, V7X_SUBLANES - (CONV_WIDTH - 1):, :])

        attn_s, value_cache = _sample_values(i, vs_rows, part_o, part_p, part_s, caches_t[-1], value_cache)
        hs = _tail(i, hs, attn_s.reshape(n_dec, GROUP_WIDTH), cbcs, sgs, ple_s, tail_w, gains,
                   tm=n_dec, parts=1)
        conv_s.append(jnp.stack([prev[1], us], axis=1))
        w_in_b = w_in_next

    to_rows = lambda c: jnp.transpose(c, (0, 1, 4, 2, 3))
    return (hp.reshape(n_seq, seq_len, D_MODEL), hs.reshape(n_dec, 1, D_MODEL),
            *[to_rows(c) for c in states_p], jnp.stack(conv_p),
            *[to_rows(c) for c in (*key_caches, value_cache)], jnp.stack(conv_s))
```

```python
import functools

import jax
import jax.numpy as jnp
import numpy as np
from jax import lax
from jax.experimental import pallas as pl
from jax.experimental.pallas import tpu as pltpu

F32 = jnp.float32
BF16 = jnp.bfloat16

D_MODEL = 1024
DEPTH = 4
PAST_LEN = 8192
HEAD_DIM = 64
HEADS_PER_GROUP = 4
GROUPS = ((128, 1), (512, 4), (2048, 16))
N_GROUPS = len(GROUPS)
GROUP_WIDTH = HEADS_PER_GROUP * HEAD_DIM
ATTN_WIDTH = N_GROUPS * GROUP_WIDTH
CONV_CHANNELS = 768
CONV_WIDTH = 3
D_FF = 4 * D_MODEL
PLE_DIM = 256
ROPE_THETA = 10000.0
BLOCK = 128
RMS_EPS = 1e-6
NEG_INF = -1e30
IN_PROJ_WIDTH = 3 * ATTN_WIDTH + 3 * CONV_CHANNELS + 2 * D_MODEL
OFF_Q, OFF_K, OFF_V = 0, ATTN_WIDTH, 2 * ATTN_WIDTH
OFF_CB = 3 * ATTN_WIDTH
OFF_CC = OFF_CB + CONV_CHANNELS
OFF_CH = OFF_CC + CONV_CHANNELS
OFF_GA = OFF_CH + CONV_CHANNELS
OFF_GB = OFF_GA + D_MODEL

V7X_LANES = 128
V7X_SUBLANES = 8
V7X_VMEM_LIMIT = 60 * 1024 * 1024
SLABS = ATTN_WIDTH // V7X_LANES
SLABS_PER_GROUP = GROUP_WIDTH // V7X_LANES
HEADS_PER_SLAB = V7X_LANES // HEAD_DIM
ROW_TILE = 512
TAIL_ROW_TILE = 512
SPAN = BLOCK * GROUPS[-1][1]
SPAN_NO_DILATION = 1024
MLP_CHUNK = 1024
ROW_PARTS = 2
SAMPLE_VALUES_BATCH = 4
STRIDE_STEP = 4
ATTN_UNROLL = {1: 4, 4: 8, 16: 4}


def _rms(x, g):
    return x * lax.rsqrt(jnp.mean(x * x, axis=-1, keepdims=True) + RMS_EPS) * g


def _dot(a, b):
    return jnp.dot(a, b, preferred_element_type=F32)


def _resident(shape, index):
    return pl.BlockSpec(shape, index, pipeline_mode=pl.Buffered(1))


def _weight_spec(w, layer):
    if w.ndim == 2:
        return _resident(w.shape, lambda *_: (0, 0))
    return _resident((None,) + w.shape[1:], lambda *_: (layer, 0, 0))


N_KEY_SIDE_CACHES = 2 * N_GROUPS - 1
N_KEY_SIDE_IN = 3 + N_KEY_SIDE_CACHES
N_KEY_SIDE_OUT = N_KEY_SIDE_CACHES + 3


def _inproj_kernel(sample, tm, tiles_per_seq, n_alias, *refs):
    if sample:
        (x_ref, cs_ref, sn_ref, g_ref, w_ref, cw_ref, p0_ref, p1_ref,
         q_ref, k_ref, v_ref, cbc_ref, sg_ref, u_ref) = refs
    else:
        x_ref, cs_ref, sn_ref, g_ref, w_ref, cw_ref = refs[:6]
        side_in = refs[6:6 + N_KEY_SIDE_IN]
        n_in = 6 + N_KEY_SIDE_IN + n_alias
        q_ref, k_ref, v_ref, cbc_ref, sg_ref, cst_ref = refs[n_in:n_in + 6]
        state_refs = refs[n_in + 6:n_in + 6 + 2 * N_GROUPS]
        side_out = refs[n_in + 6 + 2 * N_GROUPS:n_in + 6 + 2 * N_GROUPS + N_KEY_SIDE_OUT]
        u_scr = refs[-1]

    parts = 1 if sample else ROW_PARTS
    pr = tm // parts
    halo = V7X_SUBLANES
    lane = lax.broadcasted_iota(jnp.int32, (pr, V7X_LANES), 1)
    lower = (lane % HEAD_DIM) < (HEAD_DIM // 2)
    cw = cw_ref[...]

    if not sample:
        @pl.when(pl.program_id(0) % tiles_per_seq == 0)
        def _():
            u_scr[0:halo, :] = jnp.zeros((halo, CONV_CHANNELS), F32)

        _sample_keys(side_in[0], side_in[1], side_in[2], side_in[3:], side_out[:N_KEY_SIDE_CACHES],
                     *side_out[N_KEY_SIDE_CACHES:])

    xns, cbs = [], []
    for i in range(parts):
        rows = slice(i * pr, (i + 1) * pr)
        xn = _rms(x_ref[rows, :], g_ref[...]).astype(BF16)
        xns.append(xn)
        cs = cs_ref[rows, :]
        sn = sn_ref[rows, :]

        def proj(lo, width, xn=xn):
            return _dot(xn, w_ref[:, lo:lo + width])

        def rope_store(t, out_ref, scale, rows=rows, cs=cs, sn=sn):
            for c in range(SLABS):
                xc = t[:, c * V7X_LANES:(c + 1) * V7X_LANES]
                sw = jnp.where(lower, pltpu.roll(xc, V7X_LANES - HEAD_DIM // 2, 1),
                               pltpu.roll(xc, HEAD_DIM // 2, 1))
                r = xc * cs + sw * sn
                if scale is not None:
                    r = r * scale
                out_ref[c, rows, :] = r

        rope_store(proj(OFF_Q, ATTN_WIDTH), q_ref, HEAD_DIM ** -0.5)
        rope_store(proj(OFF_K, ATTN_WIDTH), k_ref, None)
        vv = proj(OFF_V, ATTN_WIDTH)
        for c in range(SLABS):
            v_ref[c, rows, :] = vv[:, c * V7X_LANES:(c + 1) * V7X_LANES]
        cbs.append(proj(OFF_CB, CONV_CHANNELS))
        u = proj(OFF_CC, CONV_CHANNELS) * proj(OFF_CH, CONV_CHANNELS)
        if sample:
            u_ref[...] = u
        else:
            u_scr[halo + i * pr:halo + (i + 1) * pr, :] = u

    for i in range(parts):
        rows = slice(i * pr, (i + 1) * pr)
        if sample:
            conv = cw[0:1] * p0_ref[...] + cw[1:2] * p1_ref[...] + cw[2:3] * u_ref[...]
        else:
            lo = halo + i * pr
            conv = (cw[0:1] * u_scr[lo - 2:lo + pr - 2, :] + cw[1:2] * u_scr[lo - 1:lo + pr - 1, :]
                    + cw[2:3] * u_scr[lo:lo + pr, :])
        cbc_ref[rows, :] = (cbs[i] * conv).astype(BF16)
        for lo, off in ((0, OFF_GA), (D_MODEL, OFF_GB)):
            gate = jax.nn.sigmoid(_dot(xns[i], w_ref[:, off:off + D_MODEL]))
            sg_ref[rows, lo:lo + D_MODEL] = gate.astype(BF16)

    if not sample:
        tail = u_scr[tm:tm + halo, :]
        u_scr[0:halo, :] = tail
        cst_ref[...] = tail
        for g, (window, _) in enumerate(GROUPS):
            keep = min(window, tm)
            for j, src in enumerate((k_ref, v_ref)):
                for half in range(SLABS_PER_GROUP):
                    last = src[g * SLABS_PER_GROUP + half, tm - keep:tm, :]
                    heads = slice(half * HEADS_PER_SLAB, (half + 1) * HEADS_PER_SLAB)
                    state_refs[2 * g + j][heads, :, :] = jnp.transpose(last).reshape(
                        HEADS_PER_SLAB, HEAD_DIM, keep)


def _inproj(layer, x, cs, sn, gains, w_in, conv_w, *, seq_len=None, prev=None, states=None, side=None):
    rows = x.shape[0]
    sample = prev is not None
    tm = rows if sample else ROW_TILE
    tiles_per_seq = 1 if sample else seq_len // tm
    grid = (rows // tm,)
    row = lambda i: (i, 0)
    slab_spec = pl.BlockSpec((SLABS, tm, V7X_LANES), lambda i: (0, i, 0))
    table_spec = pl.BlockSpec((tm, V7X_LANES), lambda i: (i % tiles_per_seq, 0))
    in_specs = [
        pl.BlockSpec((tm, D_MODEL), row),
        table_spec, table_spec,
        pl.BlockSpec((None, 1, D_MODEL), lambda i: (layer, 0, 0)),
        _weight_spec(w_in, layer),
        pl.BlockSpec((None, CONV_WIDTH, CONV_CHANNELS), lambda i: (layer, 0, 0)),
    ]
    args = [x, cs, sn, gains, w_in, conv_w]
    slab_shape = jax.ShapeDtypeStruct((SLABS, rows, V7X_LANES), F32)
    out_shape = [slab_shape, slab_shape, slab_shape,
                 jax.ShapeDtypeStruct((rows, CONV_CHANNELS), BF16),
                 jax.ShapeDtypeStruct((rows, 2 * D_MODEL), BF16)]
    out_specs = [slab_spec, slab_spec, slab_spec,
                 pl.BlockSpec((tm, CONV_CHANNELS), row),
                 pl.BlockSpec((tm, 2 * D_MODEL), row)]
    scratch = []
    aliases = {}
    n_alias = 0
    if sample:
        in_specs += [pl.BlockSpec((tm, CONV_CHANNELS), row)] * 2
        args += list(prev)
        out_shape.append(jax.ShapeDtypeStruct((rows, CONV_CHANNELS), F32))
        out_specs.append(pl.BlockSpec((tm, CONV_CHANNELS), row))
    else:
        n_seq = rows // seq_len
        out_shape.append(jax.ShapeDtypeStruct((n_seq, V7X_SUBLANES, CONV_CHANNELS), F32))
        out_specs.append(pl.BlockSpec((None, V7X_SUBLANES, CONV_CHANNELS),
                                      lambda i: (i // tiles_per_seq, 0, 0)))
        sq, sk, sv, side_caches, side_new = side
        n_dec = sq.shape[0]
        assert n_dec == grid[0] and len(side_caches) == N_KEY_SIDE_CACHES
        vec_spec = pl.BlockSpec((None, 1, ATTN_WIDTH), lambda i: (i, 0, 0))
        cache_specs = [pl.BlockSpec((None, None) + c.shape[2:], lambda i: (layer, i, 0, 0, 0))
                       for c in side_caches]
        in_specs += [vec_spec] * 3 + cache_specs
        args += [sq, sk, sv, *side_caches]
        if states is not None:
            carried = list(states) + list(side_new)
            n_alias = len(carried)
            in_specs += [pl.BlockSpec(memory_space=pl.ANY)] * n_alias
            aliases = {len(args) + j: len(out_shape) + j for j in range(n_alias)}
            args += carried
        for window, _ in GROUPS:
            keep = min(window, seq_len)
            blk = min(keep, tm)
            first_tile = tiles_per_seq - keep // blk

            def state_index(i, first_tile=first_tile):
                return (layer, i // tiles_per_seq, 0, 0,
                        jnp.maximum(i % tiles_per_seq - first_tile, 0))

            for _ in range(2):
                out_shape.append(jax.ShapeDtypeStruct(
                    (DEPTH, n_seq, HEADS_PER_GROUP, HEAD_DIM, keep), F32))
                out_specs.append(pl.BlockSpec((None, None, HEADS_PER_GROUP, HEAD_DIM, blk), state_index))
        out_shape += [jax.ShapeDtypeStruct(c.shape, F32) for c in side_caches]
        out_specs += cache_specs
        widest = GROUPS[-1][0]
        for shape in ((N_GROUPS - 1, V7X_SUBLANES, GROUP_WIDTH), (V7X_SUBLANES, widest),
                      (V7X_SUBLANES, V7X_LANES)):
            out_shape.append(jax.ShapeDtypeStruct((n_dec,) + shape, F32))
            out_specs.append(pl.BlockSpec((None,) + shape, lambda i, nd=len(shape): (i,) + (0,) * nd))
        scratch.append(pltpu.VMEM((tm + V7X_SUBLANES, CONV_CHANNELS), F32))
    return pl.pallas_call(
        functools.partial(_inproj_kernel, sample, tm, tiles_per_seq, n_alias),
        out_shape=out_shape, grid=grid, in_specs=in_specs, out_specs=out_specs,
        scratch_shapes=scratch, input_output_aliases=aliases,
        name="inproj_sample" if sample else "inproj_prompt",
        compiler_params=pltpu.CompilerParams(
            dimension_semantics=("arbitrary",), vmem_limit_bytes=V7X_VMEM_LIMIT),
    )(*args)


def _deinterleave(src, dst, tmp, dil, span):
    assert dil in (1, STRIDE_STEP, STRIDE_STEP * STRIDE_STEP)
    for c in range(SLABS_PER_GROUP):
        if dil == 1:
            dst[c] = src[c]
            continue
        seg = span // STRIDE_STEP
        stage = dst.at[c] if dil == STRIDE_STEP else tmp
        for r in range(STRIDE_STEP):
            stage[r * seg:(r + 1) * seg, :] = src[c, pl.ds(r, seg, stride=STRIDE_STEP), :]
        if dil == STRIDE_STEP:
            continue
        seg2 = seg // STRIDE_STEP
        for r in range(STRIDE_STEP):
            for r2 in range(STRIDE_STEP):
                res = r + STRIDE_STEP * r2
                dst[c, res * seg2:(res + 1) * seg2, :] = tmp[pl.ds(r * seg + r2, seg2, stride=STRIDE_STEP), :]


def _attn_kernel(dil, unroll, span, n_cast, n_other, *refs):
    q_ref, k_ref, v_ref = refs[:3]
    others = refs[3:3 + n_other]
    cast_in = refs[3 + n_other:3 + n_other + n_cast]
    n_in = 3 + n_other + n_cast
    n_res = 1 if n_other else 3
    res_refs = refs[n_in:n_in + n_res]
    cast_out = refs[n_in + n_res:n_in + n_res + n_cast]
    scratch = list(refs[n_in + n_res + n_cast:])
    kd, vd, bias = scratch[:3]
    qd = scratch[3] if dil > 1 else None
    tmp = scratch[4] if dil > STRIDE_STEP else None
    assert not (n_other and dil > 1)

    for src, dst in zip(cast_in, cast_out):
        dst[...] = src[...].astype(BF16)

    seg = span // dil
    span_idx = pl.program_id(1)
    par = span_idx % 2
    first_span = span_idx == 0

    qi = lax.broadcasted_iota(jnp.int32, (2 * BLOCK, 2 * BLOCK), 0) % BLOCK
    kj = lax.broadcasted_iota(jnp.int32, (2 * BLOCK, 2 * BLOCK), 1)
    own = kj >= BLOCK
    slack = jnp.where(own, qi - (kj - BLOCK), kj - qi)
    bias[0] = jnp.where(slack >= 0, 0.0, NEG_INF)
    bias[1] = jnp.where(jnp.where(own, slack, -1) >= 0, 0.0, NEG_INF)
    low_q = lax.broadcasted_iota(jnp.int32, (BLOCK, V7X_LANES), 1) < HEAD_DIM
    low_kv = lax.broadcasted_iota(jnp.int32, (2 * BLOCK, V7X_LANES), 1) < HEAD_DIM

    @pl.when(first_span)
    def _():
        kd[1] = jnp.zeros(kd.shape[1:], F32)
        vd[1] = jnp.zeros(vd.shape[1:], F32)

    q_rows = q_ref
    if dil > 1:
        _deinterleave(q_ref, qd, tmp, dil, span)
        q_rows = qd
    _deinterleave(k_ref, kd.at[par], tmp, dil, span)
    _deinterleave(v_ref, vd.at[par], tmp, dil, span)

    def body(blk, carry):
        res = blk % dil
        sub = blk // dil
        base = pl.multiple_of(res * seg + sub * BLOCK, BLOCK)
        inner = sub > 0
        pbuf = jnp.where(inner, par, 1 - par)
        prow = pl.multiple_of(jnp.where(inner, base - BLOCK, res * seg + seg - BLOCK), BLOCK)
        no_prev = jnp.logical_and(first_span, sub == 0).astype(jnp.int32)
        row0 = res + dil * BLOCK * sub
        out_rows = pl.ds(row0, BLOCK, stride=dil) if dil > 1 else pl.ds(base, BLOCK)
        for c in range(SLABS_PER_GROUP):
            q = q_rows[c, pl.ds(base, BLOCK), :]
            k2 = jnp.concatenate([kd[pbuf, c, pl.ds(prow, BLOCK), :],
                                  kd[par, c, pl.ds(base, BLOCK), :]], axis=0)
            v2 = jnp.concatenate([vd[pbuf, c, pl.ds(prow, BLOCK), :],
                                  vd[par, c, pl.ds(base, BLOCK), :]], axis=0)
            q2 = jnp.concatenate([jnp.where(low_q, q, 0.0), jnp.where(low_q, 0.0, q)], axis=0)
            s = lax.dot_general(q2.astype(BF16), k2.astype(BF16), (((1,), (1,)), ((), ())),
                                preferred_element_type=F32)
            s = s + bias[no_prev]
            m = jnp.max(s, axis=-1, keepdims=True)
            p = jnp.exp(s - m)
            l = jnp.sum(p, axis=-1, keepdims=True)
            pb = p.astype(BF16)
            p_cat = jnp.concatenate([pb[0:BLOCK], pb[BLOCK:2 * BLOCK]], axis=1)
            v_stack = jnp.concatenate([jnp.where(low_kv, v2, 0.0), jnp.where(low_kv, 0.0, v2)],
                                      axis=0).astype(BF16)
            o = _dot(p_cat, v_stack)
            l_slab = jnp.where(low_q, l[0:BLOCK], l[BLOCK:2 * BLOCK])
            m_slab = jnp.where(low_q, m[0:BLOCK], m[BLOCK:2 * BLOCK])
            if not n_other:
                o_ref, m_ref, l_ref = res_refs
                o_ref[c, out_rows, :] = o
                m_ref[c, out_rows, :] = m_slab
                l_ref[c, out_rows, :] = l_slab
                continue
            outs = [o] + [others[j][c, out_rows, :] for j in range(0, n_other, 3)]
            ms = [m_slab] + [others[j][c, out_rows, :] for j in range(1, n_other, 3)]
            ls = [l_slab] + [others[j][c, out_rows, :] for j in range(2, n_other, 3)]
            mx = functools.reduce(jnp.maximum, ms)
            es = [jnp.exp(mg - mx) for mg in ms]
            num = functools.reduce(lambda x, y: x + y, [e * og for e, og in zip(es, outs)])
            den = functools.reduce(lambda x, y: x + y, [e * lg for e, lg in zip(es, ls)])
            res_refs[0][out_rows, c * V7X_LANES:(c + 1) * V7X_LANES] = (num / den).astype(BF16)
        return carry

    lax.fori_loop(0, span // BLOCK, body, 0, unroll=unroll)


def _attn_group(group, q, k, v, n_seq, seq_len, casts, others=()):
    dil = GROUPS[group][1]
    rows = q.shape[1]
    span = SPAN if dil > 1 else SPAN_NO_DILATION
    spans_per_seq = seq_len // span
    steps = n_seq * spans_per_seq
    step = lambda b, n: b * spans_per_seq + n
    cast_in_specs, cast_out_specs, cast_shapes = [], [], []
    for w, layer in casts:
        blk = (w.shape[1] // steps, w.shape[2])
        cast_in_specs.append(pl.BlockSpec((None,) + blk, lambda b, n, layer=layer: (layer, step(b, n), 0)))
        cast_out_specs.append(pl.BlockSpec(blk, lambda b, n: (step(b, n), 0)))
        cast_shapes.append(jax.ShapeDtypeStruct(w.shape[1:], BF16))
    cur_spec = pl.BlockSpec((SLABS_PER_GROUP, span, V7X_LANES), lambda b, n: (group, step(b, n), 0))
    slab_spec = pl.BlockSpec((SLABS_PER_GROUP, span, V7X_LANES), lambda b, n: (0, step(b, n), 0))
    if others:
        res_shapes = [jax.ShapeDtypeStruct((rows, GROUP_WIDTH), BF16)]
        res_specs = [pl.BlockSpec((span, GROUP_WIDTH), lambda b, n: (step(b, n), 0))]
    else:
        res_shapes = [jax.ShapeDtypeStruct((SLABS_PER_GROUP, rows, V7X_LANES), F32)] * 3
        res_specs = [slab_spec] * 3
    span_buf = (SLABS_PER_GROUP, span, V7X_LANES)
    scratch = [pltpu.VMEM((2,) + span_buf, F32), pltpu.VMEM((2,) + span_buf, F32),
               pltpu.VMEM((2, 2 * BLOCK, 2 * BLOCK), F32)]
    if dil > 1:
        scratch.append(pltpu.VMEM(span_buf, F32))
    if dil > STRIDE_STEP:
        scratch.append(pltpu.VMEM((span, V7X_LANES), F32))
    return pl.pallas_call(
        functools.partial(_attn_kernel, dil, ATTN_UNROLL[dil], span, len(casts), len(others)),
        out_shape=res_shapes + cast_shapes, grid=(n_seq, spans_per_seq),
        in_specs=[cur_spec, cur_spec, cur_spec] + [slab_spec] * len(others) + cast_in_specs,
        out_specs=res_specs + cast_out_specs,
        scratch_shapes=scratch,
        name="attn_prompt_d%d" % dil,
        compiler_params=pltpu.CompilerParams(
            dimension_semantics=("arbitrary", "arbitrary"), vmem_limit_bytes=V7X_VMEM_LIMIT),
    )(q, k, v, *others, *[w for w, _ in casts])


def _tail_kernel(parts, *refs):
    (h_ref, attn_ref, cbc_ref, sg_ref, ple_ref, wa_ref, wc_ref, wo_ref, wu_ref, wd_ref, wg_ref, wp_ref,
     gains_ref, out_ref) = refs

    pr = h_ref.shape[0] // parts
    row_slices = [slice(i * pr, (i + 1) * pr) for i in range(parts)]

    mixed = []
    for rows in row_slices:
        a = _dot(attn_ref[rows, :].astype(BF16), wa_ref[...])
        c = _dot(cbc_ref[rows, :], wc_ref[...])
        sg = sg_ref[rows, :].astype(F32)
        mix = sg[:, 0:D_MODEL] * a + sg[:, D_MODEL:2 * D_MODEL] * c
        mixed.append(_dot(mix.astype(BF16), wo_ref[...]))

    embeds = [_dot(ple_ref[rows, :].astype(BF16), wp_ref[...]) for rows in row_slices]

    for rows, mo in zip(row_slices, mixed):
        out_ref[rows, :] = h_ref[rows, :] + _rms(mo, gains_ref[1:2, :])

    mlp = []
    for rows in row_slices:
        xn = _rms(out_ref[rows, :], gains_ref[2:3, :]).astype(BF16)
        acc = None
        for lo in range(0, D_FF, MLP_CHUNK):
            f = _dot(xn, wu_ref[:, lo:lo + MLP_CHUNK])
            f = jnp.square(jnp.maximum(f, 0.0)).astype(BF16)
            t = _dot(f, wd_ref[lo:lo + MLP_CHUNK, :])
            acc = t if acc is None else acc + t
        mlp.append(acc)

    for rows, acc in zip(row_slices, mlp):
        out_ref[rows, :] = out_ref[rows, :] + _rms(acc, gains_ref[3:4, :])

    for rows, embed in zip(row_slices, embeds):
        h = out_ref[rows, :]
        gate = jax.nn.sigmoid(_dot(_rms(h, gains_ref[4:5, :]).astype(BF16), wg_ref[...]))
        out_ref[rows, :] = h + gate * embed


def _tail(layer, h, attn, cbc, sg, ple, weights, gains, *, tm, parts):
    rows = h.shape[0]
    row = lambda i: (i, 0)
    in_specs = [
        pl.BlockSpec((tm, D_MODEL), row),
        pl.BlockSpec((tm, GROUP_WIDTH), row),
        pl.BlockSpec((tm, CONV_CHANNELS), row),
        pl.BlockSpec((tm, 2 * D_MODEL), row),
        pl.BlockSpec((None, tm, PLE_DIM), lambda i: (layer, i, 0)),
    ]
    in_specs += [_weight_spec(w, layer) for w in weights]
    in_specs += [pl.BlockSpec((None,) + gains.shape[1:], lambda i: (layer, 0, 0))]
    return pl.pallas_call(
        functools.partial(_tail_kernel, parts),
        out_shape=jax.ShapeDtypeStruct((rows, D_MODEL), F32),
        grid=(rows // tm,), in_specs=in_specs,
        out_specs=pl.BlockSpec((tm, D_MODEL), row),
        name="tail_prompt" if rows > tm else "tail_sample",
        compiler_params=pltpu.CompilerParams(
            dimension_semantics=("arbitrary",), vmem_limit_bytes=V7X_VMEM_LIMIT),
    )(h, attn, cbc, sg, ple, *weights, gains)


def _shift_in(old, new_col):
    width = old.shape[1]
    tiles = width // V7X_LANES
    lane = lax.broadcasted_iota(jnp.int32, (old.shape[0], V7X_LANES), 1)
    last = lane == V7X_LANES - 1
    rot = [pltpu.roll(old[:, t * V7X_LANES:(t + 1) * V7X_LANES], V7X_LANES - 1, 1)
           for t in range(tiles)]
    out = [jnp.where(last, rot[t + 1] if t + 1 < tiles else new_col, rot[t]) for t in range(tiles)]
    return out[0] if tiles == 1 else jnp.concatenate(out, axis=1)


def _own_head_mask():
    head_row = lax.broadcasted_iota(jnp.int32, (V7X_SUBLANES, GROUP_WIDTH), 0)
    head_lane = lax.broadcasted_iota(jnp.int32, (V7X_SUBLANES, GROUP_WIDTH), 1) // HEAD_DIM
    return head_row == head_lane


def _new_column(row):
    return jnp.transpose(jnp.broadcast_to(row, (V7X_LANES, GROUP_WIDTH)))


def _values_out(p, p_new, vt, v_new):
    return lax.dot_general(p.astype(BF16), vt.astype(BF16), (((1,), (1,)), ((), ())),
                           preferred_element_type=F32) + p_new * v_new


def _sample_keys(q_ref, kn_ref, vn_ref, caches, outs, part_o_ref, part_p_ref, part_s_ref):
    q = q_ref[...]
    kn = kn_ref[...]
    vn = vn_ref[...]
    own_head = _own_head_mask()
    stats = []
    for g, (window, dil) in enumerate(GROUPS):
        sl = slice(g * GROUP_WIDTH, (g + 1) * GROUP_WIDTH)
        kt = caches[2 * g][...].reshape(GROUP_WIDTH, window)
        q_heads = jnp.where(own_head, jnp.broadcast_to(q[:, sl], own_head.shape), 0.0)
        s = _dot(q_heads.astype(BF16), kt.astype(BF16))
        pos = lax.broadcasted_iota(jnp.int32, s.shape, 1)
        s = jnp.where(pos % dil == 0, s, NEG_INF)
        s_new = jnp.sum(q_heads * kn[:, sl], axis=-1, keepdims=True)
        m = jnp.maximum(jnp.max(s, axis=-1, keepdims=True), s_new)
        p = jnp.exp(s - m)
        p_new = jnp.exp(s_new - m)
        l = jnp.sum(p, axis=-1, keepdims=True) + p_new
        stats += [m, l]
        outs[2 * g][...] = _shift_in(kt, _new_column(kn[:, sl])).reshape(HEADS_PER_GROUP, HEAD_DIM, window)
        if g + 1 < N_GROUPS:
            vt = caches[2 * g + 1][...].reshape(GROUP_WIDTH, window)
            part_o_ref[g] = _values_out(p, p_new, vt, vn[:, sl]) / l
            outs[2 * g + 1][...] = _shift_in(vt, _new_column(vn[:, sl])).reshape(
                HEADS_PER_GROUP, HEAD_DIM, window)
        else:
            part_p_ref[...] = p
            stats.append(p_new)
    lane = lax.broadcasted_iota(jnp.int32, (V7X_SUBLANES, V7X_LANES), 1)
    packed = jnp.zeros((V7X_SUBLANES, V7X_LANES), F32)
    for j, col in enumerate(stats):
        packed = jnp.where(lane == j, col, packed)
    part_s_ref[...] = packed


def _sample_values_kernel(n_alias, vn_ref, part_o_ref, part_p_ref, part_s_ref, cv_ref, *rest):
    attn_ref, nv_ref = rest[n_alias], rest[n_alias + 1]
    g = N_GROUPS - 1
    window = GROUPS[g][0]
    for b in range(vn_ref.shape[0]):
        stats = part_s_ref[b]
        col = lambda j, stats=stats: stats[:, j:j + 1]
        v_new = vn_ref[b][:, g * GROUP_WIDTH:(g + 1) * GROUP_WIDTH]
        vt = cv_ref[b].reshape(GROUP_WIDTH, window)
        parts = [(part_o_ref[b, j], col(2 * j), col(2 * j + 1)) for j in range(g)]
        parts.append((_values_out(part_p_ref[b], col(2 * N_GROUPS), vt, v_new) / col(2 * g + 1),
                      col(2 * g), col(2 * g + 1)))
        mx = jnp.maximum(jnp.maximum(parts[0][1], parts[1][1]), parts[2][1])
        ws = [l * jnp.exp(m - mx) for _, m, l in parts]
        num = ws[0] * parts[0][0] + ws[1] * parts[1][0] + ws[2] * parts[2][0]
        merged = num / (ws[0] + ws[1] + ws[2])
        attn_ref[b] = jnp.sum(jnp.where(_own_head_mask(), merged, 0.0), axis=0, keepdims=True)
        nv_ref[b] = _shift_in(vt, _new_column(v_new)).reshape(HEADS_PER_GROUP, HEAD_DIM, window)


def _sample_values(layer, v_new, part_o, part_p, part_s, cache_v, new_cache_v):
    batch = v_new.shape[0]
    bb = SAMPLE_VALUES_BATCH
    per_seq = lambda a: pl.BlockSpec((bb,) + a.shape[1:], lambda b: (b,) + (0,) * (a.ndim - 1))
    cache_spec = pl.BlockSpec((None, bb) + cache_v.shape[2:], lambda b: (layer, b, 0, 0, 0))
    in_specs = [per_seq(v_new), per_seq(part_o), per_seq(part_p), per_seq(part_s), cache_spec]
    args = [v_new, part_o, part_p, part_s, cache_v]
    aliases = {}
    n_alias = 0
    if new_cache_v is not None:
        n_alias = 1
        in_specs.append(pl.BlockSpec(memory_space=pl.ANY))
        aliases = {len(args): 1}
        args.append(new_cache_v)
    return pl.pallas_call(
        functools.partial(_sample_values_kernel, n_alias),
        out_shape=[jax.ShapeDtypeStruct((batch, 1, GROUP_WIDTH), F32),
                   jax.ShapeDtypeStruct(cache_v.shape, F32)],
        grid=(batch // bb,), in_specs=in_specs,
        out_specs=[pl.BlockSpec((bb, 1, GROUP_WIDTH), lambda b: (b, 0, 0)), cache_spec],
        input_output_aliases=aliases, name="attn_sample_values",
        compiler_params=pltpu.CompilerParams(
            dimension_semantics=("arbitrary",), vmem_limit_bytes=V7X_VMEM_LIMIT),
    )(*args)


def _rope_tables(pos):
    half = HEAD_DIM // 2
    inv = np.power(ROPE_THETA, -2.0 * np.arange(half, dtype=np.float64) / HEAD_DIM)
    ang = np.asarray(pos, dtype=np.float64)[:, None] * inv[None, :]
    cos, sin = np.cos(ang), np.sin(ang)
    reps = V7X_LANES // HEAD_DIM
    return (np.concatenate([cos, cos] * reps, axis=1).astype(np.float32),
            np.concatenate([-sin, sin] * reps, axis=1).astype(np.float32))


def kernel(x_prompt, x_sample, p_prompt, p_sample, cache_k_w128, cache_v_w128, cache_k_w512, cache_v_w512, cache_k_w2048, cache_v_w2048, state_conv, w_in, conv_w, w_attn_out, w_conv_out, w_o, g_pre_mix, g_post_mix, w_up, w_down, g_pre_mlp, g_post_mlp, g_ple, w_ple_gate, w_ple_proj):
    n_seq, seq_len, _ = x_prompt.shape
    n_dec = x_sample.shape[0]
    rows = n_seq * seq_len

    w_in_b = w_in[0].astype(BF16)
    w_attn_out_b = w_attn_out.astype(BF16)
    w_ple_proj_b = w_ple_proj.astype(BF16)
    gains = jnp.stack([g_pre_mix, g_post_mix, g_pre_mlp, g_post_mlp, g_ple], axis=1)
    g_in = g_pre_mix[:, None, :]

    cs_p, sn_p = _rope_tables(np.arange(seq_len))
    cs_s, sn_s = _rope_tables(np.full((n_dec,), PAST_LEN))

    caches = (cache_k_w128, cache_v_w128, cache_k_w512, cache_v_w512, cache_k_w2048, cache_v_w2048)
    caches_t = [jnp.transpose(c, (0, 1, 3, 4, 2)) for c in caches]

    hp = x_prompt.reshape(rows, D_MODEL)
    hs = x_sample.reshape(n_dec, D_MODEL)
    ple_p = p_prompt.reshape(DEPTH, rows, PLE_DIM)
    ple_s = p_sample.reshape(DEPTH, n_dec, PLE_DIM)

    conv_p, conv_s = [], []
    states_p = None
    key_caches = None
    value_cache = None
    n_states = 2 * N_GROUPS
    as_rows = lambda t: jnp.transpose(t, (1, 0, 2)).reshape(n_dec, 1, ATTN_WIDTH)
    for i in range(DEPTH):
        prev = (state_conv[i, :, 0, :], state_conv[i, :, 1, :])
        qs, ks, vs, cbcs, sgs, us = _inproj(i, hs, cs_s, sn_s, g_in, w_in_b, conv_w, prev=prev)
        vs_rows = as_rows(vs)
        side = (as_rows(qs), as_rows(ks), vs_rows, caches_t[:N_KEY_SIDE_CACHES], key_caches)
        res = _inproj(i, hp, cs_p, sn_p, g_in, w_in_b, conv_w, seq_len=seq_len, states=states_p,
                      side=side)
        q, k, v, cbc, sg, cst = res[:6]
        states_p = res[6:6 + n_states]
        key_caches = res[6 + n_states:6 + n_states + N_KEY_SIDE_CACHES]
        part_o, part_p, part_s = res[6 + n_states + N_KEY_SIDE_CACHES:]
        casts = ([(w_up, i), (w_o, i), (w_ple_gate, i), (w_conv_out, i)], [(w_down, i)],
                 [(w_in, i + 1)] if i + 1 < DEPTH else [])
        res16 = _attn_group(2, q, k, v, n_seq, seq_len, casts[2])
        res4 = _attn_group(1, q, k, v, n_seq, seq_len, casts[1])
        res1 = _attn_group(0, q, k, v, n_seq, seq_len, casts[0], others=res4[:3] + res16[:3])
        attn_p, w_up_b, w_o_b, w_ple_gate_b, w_conv_out_b = res1
        w_down_b = res4[3]
        w_in_next = res16[3] if len(res16) > 3 else None
        tail_w = [w_attn_out_b, w_conv_out_b, w_o_b, w_up_b, w_down_b, w_ple_gate_b, w_ple_proj_b]
        hp = _tail(i, hp, attn_p, cbc, sg, ple_p, tail_w, gains, tm=TAIL_ROW_TILE, parts=ROW_PARTS)
        conv_p.append(cst[:, V7X_SUBLANES - (CONV_WIDTH - 1):, :])

        attn_s, value_cache = _sample_values(i, vs_rows, part_o, part_p, part_s, caches_t[-1], value_cache)
        hs = _tail(i, hs, attn_s.reshape(n_dec, GROUP_WIDTH), cbcs, sgs, ple_s, tail_w, gains,
                   tm=n_dec, parts=1)
        conv_s.append(jnp.stack([prev[1], us], axis=1))
        w_in_b = w_in_next

    to_rows = lambda c: jnp.transpose(c, (0, 1, 4, 2, 3))
    return (hp.reshape(n_seq, seq_len, D_MODEL), hs.reshape(n_dec, 1, D_MODEL),
            *[to_rows(c) for c in states_p], jnp.stack(conv_p),
            *[to_rows(c) for c in (*key_caches, value_cache)], jnp.stack(conv_s))
```

```python
import functools

import jax
import jax.numpy as jnp
import numpy as np
from jax import lax
from jax.experimental import pallas as pl
from jax.experimental.pallas import tpu as pltpu

F32 = jnp.float32
BF16 = jnp.bfloat16

D_MODEL = 1024
DEPTH = 4
PAST_LEN = 8192
HEAD_DIM = 64
HEADS_PER_GROUP = 4
GROUPS = ((128, 1), (512, 4), (2048, 16))
N_GROUPS = len(GROUPS)
GROUP_WIDTH = HEADS_PER_GROUP * HEAD_DIM
ATTN_WIDTH = N_GROUPS * GROUP_WIDTH
CONV_CHANNELS = 768
CONV_WIDTH = 3
D_FF = 4 * D_MODEL
PLE_DIM = 256
ROPE_THETA = 10000.0
BLOCK = 128
RMS_EPS = 1e-6
NEG_INF = -1e30
IN_PROJ_WIDTH = 3 * ATTN_WIDTH + 3 * CONV_CHANNELS + 2 * D_MODEL
OFF_Q, OFF_K, OFF_V = 0, ATTN_WIDTH, 2 * ATTN_WIDTH
OFF_CB = 3 * ATTN_WIDTH
OFF_CC = OFF_CB + CONV_CHANNELS
OFF_CH = OFF_CC + CONV_CHANNELS
OFF_GA = OFF_CH + CONV_CHANNELS
OFF_GB = OFF_GA + D_MODEL

V7X_LANES = 128
V7X_SUBLANES = 8
V7X_VMEM_LIMIT = 60 * 1024 * 1024
SLABS = ATTN_WIDTH // V7X_LANES
SLABS_PER_GROUP = GROUP_WIDTH // V7X_LANES
HEADS_PER_SLAB = V7X_LANES // HEAD_DIM
ROW_TILE = 512
TAIL_ROW_TILE = 512
SPAN = BLOCK * GROUPS[-1][1]
SPAN_NO_DILATION = 1024
MLP_CHUNK = 1024
ROW_PARTS = 2
SAMPLE_VALUES_BATCH = 4
STRIDE_STEP = 4
ATTN_UNROLL = {1: 8, 4: 8, 16: 8}


def _rms(x, g):
    return x * lax.rsqrt(jnp.mean(x * x, axis=-1, keepdims=True) + RMS_EPS) * g


def _dot(a, b):
    return jnp.dot(a, b, preferred_element_type=F32)


def _resident(shape, index):
    return pl.BlockSpec(shape, index, pipeline_mode=pl.Buffered(1))


def _weight_spec(w, layer):
    if w.ndim == 2:
        return _resident(w.shape, lambda *_: (0, 0))
    return _resident((None,) + w.shape[1:], lambda *_: (layer, 0, 0))


N_KEY_SIDE_CACHES = 2 * N_GROUPS - 1
N_KEY_SIDE_IN = 3 + N_KEY_SIDE_CACHES
N_KEY_SIDE_OUT = N_KEY_SIDE_CACHES + 3


def _inproj_kernel(sample, tm, tiles_per_seq, n_alias, *refs):
    if sample:
        (x_ref, cs_ref, sn_ref, g_ref, w_ref, cw_ref, p0_ref, p1_ref,
         q_ref, k_ref, v_ref, cbc_ref, sg_ref, u_ref) = refs
    else:
        x_ref, cs_ref, sn_ref, g_ref, w_ref, cw_ref = refs[:6]
        side_in = refs[6:6 + N_KEY_SIDE_IN]
        n_in = 6 + N_KEY_SIDE_IN + n_alias
        q_ref, k_ref, v_ref, cbc_ref, sg_ref, cst_ref = refs[n_in:n_in + 6]
        state_refs = refs[n_in + 6:n_in + 6 + 2 * N_GROUPS]
        side_out = refs[n_in + 6 + 2 * N_GROUPS:n_in + 6 + 2 * N_GROUPS + N_KEY_SIDE_OUT]
        u_scr = refs[-1]

    parts = 1 if sample else ROW_PARTS
    pr = tm // parts
    halo = V7X_SUBLANES
    lane = lax.broadcasted_iota(jnp.int32, (pr, V7X_LANES), 1)
    lower = (lane % HEAD_DIM) < (HEAD_DIM // 2)
    cw = cw_ref[...]

    if not sample:
        @pl.when(pl.program_id(0) % tiles_per_seq == 0)
        def _():
            u_scr[0:halo, :] = jnp.zeros((halo, CONV_CHANNELS), F32)

        _sample_keys(side_in[0], side_in[1], side_in[2], side_in[3:], side_out[:N_KEY_SIDE_CACHES],
                     *side_out[N_KEY_SIDE_CACHES:])

    xns, cbs = [], []
    for i in range(parts):
        rows = slice(i * pr, (i + 1) * pr)
        xn = _rms(x_ref[rows, :], g_ref[...]).astype(BF16)
        xns.append(xn)
        cs = cs_ref[rows, :]
        sn = sn_ref[rows, :]

        def proj(lo, width, xn=xn):
            return _dot(xn, w_ref[:, lo:lo + width])

        def rope_store(t, out_ref, scale, rows=rows, cs=cs, sn=sn):
            for c in range(SLABS):
                xc = t[:, c * V7X_LANES:(c + 1) * V7X_LANES]
                sw = jnp.where(lower, pltpu.roll(xc, V7X_LANES - HEAD_DIM // 2, 1),
                               pltpu.roll(xc, HEAD_DIM // 2, 1))
                r = xc * cs + sw * sn
                if scale is not None:
                    r = r * scale
                out_ref[c, rows, :] = r

        rope_store(proj(OFF_Q, ATTN_WIDTH), q_ref, HEAD_DIM ** -0.5)
        rope_store(proj(OFF_K, ATTN_WIDTH), k_ref, None)
        vv = proj(OFF_V, ATTN_WIDTH)
        for c in range(SLABS):
            v_ref[c, rows, :] = vv[:, c * V7X_LANES:(c + 1) * V7X_LANES]
        cbs.append(proj(OFF_CB, CONV_CHANNELS))
        u = proj(OFF_CC, CONV_CHANNELS) * proj(OFF_CH, CONV_CHANNELS)
        if sample:
            u_ref[...] = u
        else:
            u_scr[halo + i * pr:halo + (i + 1) * pr, :] = u

    for i in range(parts):
        rows = slice(i * pr, (i + 1) * pr)
        if sample:
            conv = cw[0:1] * p0_ref[...] + cw[1:2] * p1_ref[...] + cw[2:3] * u_ref[...]
        else:
            lo = halo + i * pr
            conv = (cw[0:1] * u_scr[lo - 2:lo + pr - 2, :] + cw[1:2] * u_scr[lo - 1:lo + pr - 1, :]
                    + cw[2:3] * u_scr[lo:lo + pr, :])
        cbc_ref[rows, :] = (cbs[i] * conv).astype(BF16)
        for lo, off in ((0, OFF_GA), (D_MODEL, OFF_GB)):
            gate = jax.nn.sigmoid(_dot(xns[i], w_ref[:, off:off + D_MODEL]))
            sg_ref[rows, lo:lo + D_MODEL] = gate.astype(BF16)

    if not sample:
        tail = u_scr[tm:tm + halo, :]
        u_scr[0:halo, :] = tail
        cst_ref[...] = tail
        for g, (window, _) in enumerate(GROUPS):
            keep = min(window, tm)
            for j, src in enumerate((k_ref, v_ref)):
                for half in range(SLABS_PER_GROUP):
                    last = src[g * SLABS_PER_GROUP + half, tm - keep:tm, :]
                    heads = slice(half * HEADS_PER_SLAB, (half + 1) * HEADS_PER_SLAB)
                    state_refs[2 * g + j][heads, :, :] = jnp.transpose(last).reshape(
                        HEADS_PER_SLAB, HEAD_DIM, keep)


def _inproj(layer, x, cs, sn, gains, w_in, conv_w, *, seq_len=None, prev=None, states=None, side=None):
    rows = x.shape[0]
    sample = prev is not None
    tm = rows if sample else ROW_TILE
    tiles_per_seq = 1 if sample else seq_len // tm
    grid = (rows // tm,)
    row = lambda i: (i, 0)
    slab_spec = pl.BlockSpec((SLABS, tm, V7X_LANES), lambda i: (0, i, 0))
    table_spec = pl.BlockSpec((tm, V7X_LANES), lambda i: (i % tiles_per_seq, 0))
    in_specs = [
        pl.BlockSpec((tm, D_MODEL), row),
        table_spec, table_spec,
        pl.BlockSpec((None, 1, D_MODEL), lambda i: (layer, 0, 0)),
        _weight_spec(w_in, layer),
        pl.BlockSpec((None, CONV_WIDTH, CONV_CHANNELS), lambda i: (layer, 0, 0)),
    ]
    args = [x, cs, sn, gains, w_in, conv_w]
    slab_shape = jax.ShapeDtypeStruct((SLABS, rows, V7X_LANES), F32)
    out_shape = [slab_shape, slab_shape, slab_shape,
                 jax.ShapeDtypeStruct((rows, CONV_CHANNELS), BF16),
                 jax.ShapeDtypeStruct((rows, 2 * D_MODEL), BF16)]
    out_specs = [slab_spec, slab_spec, slab_spec,
                 pl.BlockSpec((tm, CONV_CHANNELS), row),
                 pl.BlockSpec((tm, 2 * D_MODEL), row)]
    scratch = []
    aliases = {}
    n_alias = 0
    if sample:
        in_specs += [pl.BlockSpec((tm, CONV_CHANNELS), row)] * 2
        args += list(prev)
        out_shape.append(jax.ShapeDtypeStruct((rows, CONV_CHANNELS), F32))
        out_specs.append(pl.BlockSpec((tm, CONV_CHANNELS), row))
    else:
        n_seq = rows // seq_len
        out_shape.append(jax.ShapeDtypeStruct((n_seq, V7X_SUBLANES, CONV_CHANNELS), F32))
        out_specs.append(pl.BlockSpec((None, V7X_SUBLANES, CONV_CHANNELS),
                                      lambda i: (i // tiles_per_seq, 0, 0)))
        sq, sk, sv, side_caches, side_new = side
        n_dec = sq.shape[0]
        assert n_dec == grid[0] and len(side_caches) == N_KEY_SIDE_CACHES
        vec_spec = pl.BlockSpec((None, 1, ATTN_WIDTH), lambda i: (i, 0, 0))
        cache_specs = [pl.BlockSpec((None, None) + c.shape[2:], lambda i: (layer, i, 0, 0, 0))
                       for c in side_caches]
        in_specs += [vec_spec] * 3 + cache_specs
        args += [sq, sk, sv, *side_caches]
        if states is not None:
            carried = list(states) + list(side_new)
            n_alias = len(carried)
            in_specs += [pl.BlockSpec(memory_space=pl.ANY)] * n_alias
            aliases = {len(args) + j: len(out_shape) + j for j in range(n_alias)}
            args += carried
        for window, _ in GROUPS:
            keep = min(window, seq_len)
            blk = min(keep, tm)
            first_tile = tiles_per_seq - keep // blk

            def state_index(i, first_tile=first_tile):
                return (layer, i // tiles_per_seq, 0, 0,
                        jnp.maximum(i % tiles_per_seq - first_tile, 0))

            for _ in range(2):
                out_shape.append(jax.ShapeDtypeStruct(
                    (DEPTH, n_seq, HEADS_PER_GROUP, HEAD_DIM, keep), F32))
                out_specs.append(pl.BlockSpec((None, None, HEADS_PER_GROUP, HEAD_DIM, blk), state_index))
        out_shape += [jax.ShapeDtypeStruct(c.shape, F32) for c in side_caches]
        out_specs += cache_specs
        widest = GROUPS[-1][0]
        for shape in ((N_GROUPS - 1, V7X_SUBLANES, GROUP_WIDTH), (V7X_SUBLANES, widest),
                      (V7X_SUBLANES, V7X_LANES)):
            out_shape.append(jax.ShapeDtypeStruct((n_dec,) + shape, F32))
            out_specs.append(pl.BlockSpec((None,) + shape, lambda i, nd=len(shape): (i,) + (0,) * nd))
        scratch.append(pltpu.VMEM((tm + V7X_SUBLANES, CONV_CHANNELS), F32))
    return pl.pallas_call(
        functools.partial(_inproj_kernel, sample, tm, tiles_per_seq, n_alias),
        out_shape=out_shape, grid=grid, in_specs=in_specs, out_specs=out_specs,
        scratch_shapes=scratch, input_output_aliases=aliases,
        name="inproj_sample" if sample else "inproj_prompt",
        compiler_params=pltpu.CompilerParams(
            dimension_semantics=("arbitrary",), vmem_limit_bytes=V7X_VMEM_LIMIT),
    )(*args)


def _deinterleave(src, dst, tmp, dil, span):
    assert dil in (1, STRIDE_STEP, STRIDE_STEP * STRIDE_STEP)
    for c in range(SLABS_PER_GROUP):
        if dil == 1:
            dst[c] = src[c]
            continue
        seg = span // STRIDE_STEP
        stage = dst.at[c] if dil == STRIDE_STEP else tmp
        for r in range(STRIDE_STEP):
            stage[r * seg:(r + 1) * seg, :] = src[c, pl.ds(r, seg, stride=STRIDE_STEP), :]
        if dil == STRIDE_STEP:
            continue
        seg2 = seg // STRIDE_STEP
        for r in range(STRIDE_STEP):
            for r2 in range(STRIDE_STEP):
                res = r + STRIDE_STEP * r2
                dst[c, res * seg2:(res + 1) * seg2, :] = tmp[pl.ds(r * seg + r2, seg2, stride=STRIDE_STEP), :]


def _attn_kernel(dil, unroll, span, n_cast, n_other, *refs):
    q_ref, k_ref, v_ref = refs[:3]
    others = refs[3:3 + n_other]
    cast_in = refs[3 + n_other:3 + n_other + n_cast]
    n_in = 3 + n_other + n_cast
    n_res = 1 if n_other else 3
    res_refs = refs[n_in:n_in + n_res]
    cast_out = refs[n_in + n_res:n_in + n_res + n_cast]
    scratch = list(refs[n_in + n_res + n_cast:])
    kd, vd, bias = scratch[:3]
    qd = scratch[3] if dil > 1 else None
    tmp = scratch[4] if dil > STRIDE_STEP else None
    assert not (n_other and dil > 1)

    for src, dst in zip(cast_in, cast_out):
        dst[...] = src[...].astype(BF16)

    seg = span // dil
    span_idx = pl.program_id(1)
    par = span_idx % 2
    first_span = span_idx == 0

    qi = lax.broadcasted_iota(jnp.int32, (2 * BLOCK, 2 * BLOCK), 0) % BLOCK
    kj = lax.broadcasted_iota(jnp.int32, (2 * BLOCK, 2 * BLOCK), 1)
    own = kj >= BLOCK
    slack = jnp.where(own, qi - (kj - BLOCK), kj - qi)
    bias[0] = jnp.where(slack >= 0, 0.0, NEG_INF)
    bias[1] = jnp.where(jnp.where(own, slack, -1) >= 0, 0.0, NEG_INF)
    low_q = lax.broadcasted_iota(jnp.int32, (BLOCK, V7X_LANES), 1) < HEAD_DIM
    low_kv = lax.broadcasted_iota(jnp.int32, (2 * BLOCK, V7X_LANES), 1) < HEAD_DIM

    @pl.when(first_span)
    def _():
        kd[1] = jnp.zeros(kd.shape[1:], F32)
        vd[1] = jnp.zeros(vd.shape[1:], F32)

    q_rows = q_ref
    if dil > 1:
        _deinterleave(q_ref, qd, tmp, dil, span)
        q_rows = qd
    _deinterleave(k_ref, kd.at[par], tmp, dil, span)
    _deinterleave(v_ref, vd.at[par], tmp, dil, span)

    def body(blk, carry):
        res = blk % dil
        sub = blk // dil
        base = pl.multiple_of(res * seg + sub * BLOCK, BLOCK)
        inner = sub > 0
        pbuf = jnp.where(inner, par, 1 - par)
        prow = pl.multiple_of(jnp.where(inner, base - BLOCK, res * seg + seg - BLOCK), BLOCK)
        no_prev = jnp.logical_and(first_span, sub == 0).astype(jnp.int32)
        row0 = res + dil * BLOCK * sub
        out_rows = pl.ds(row0, BLOCK, stride=dil) if dil > 1 else pl.ds(base, BLOCK)
        for c in range(SLABS_PER_GROUP):
            q = q_rows[c, pl.ds(base, BLOCK), :]
            k2 = jnp.concatenate([kd[pbuf, c, pl.ds(prow, BLOCK), :],
                                  kd[par, c, pl.ds(base, BLOCK), :]], axis=0)
            v2 = jnp.concatenate([vd[pbuf, c, pl.ds(prow, BLOCK), :],
                                  vd[par, c, pl.ds(base, BLOCK), :]], axis=0)
            q2 = jnp.concatenate([jnp.where(low_q, q, 0.0), jnp.where(low_q, 0.0, q)], axis=0)
            s = lax.dot_general(q2.astype(BF16), k2.astype(BF16), (((1,), (1,)), ((), ())),
                                preferred_element_type=F32)
            s = s + bias[no_prev]
            m = jnp.max(s, axis=-1, keepdims=True)
            p = jnp.exp(s - m)
            l = jnp.sum(p, axis=-1, keepdims=True)
            pb = p.astype(BF16)
            p_cat = jnp.concatenate([pb[0:BLOCK], pb[BLOCK:2 * BLOCK]], axis=1)
            v_stack = jnp.concatenate([jnp.where(low_kv, v2, 0.0), jnp.where(low_kv, 0.0, v2)],
                                      axis=0).astype(BF16)
            o = _dot(p_cat, v_stack)
            l_slab = jnp.where(low_q, l[0:BLOCK], l[BLOCK:2 * BLOCK])
            m_slab = jnp.where(low_q, m[0:BLOCK], m[BLOCK:2 * BLOCK])
            if not n_other:
                o_ref, m_ref, l_ref = res_refs
                o_ref[c, out_rows, :] = o
                m_ref[c, out_rows, :] = m_slab
                l_ref[c, out_rows, :] = l_slab
                continue
            outs = [o] + [others[j][c, out_rows, :] for j in range(0, n_other, 3)]
            ms = [m_slab] + [others[j][c, out_rows, :] for j in range(1, n_other, 3)]
            ls = [l_slab] + [others[j][c, out_rows, :] for j in range(2, n_other, 3)]
            mx = functools.reduce(jnp.maximum, ms)
            es = [jnp.exp(mg - mx) for mg in ms]
            num = functools.reduce(lambda x, y: x + y, [e * og for e, og in zip(es, outs)])
            den = functools.reduce(lambda x, y: x + y, [e * lg for e, lg in zip(es, ls)])
            res_refs[0][out_rows, c * V7X_LANES:(c + 1) * V7X_LANES] = (num / den).astype(BF16)
        return carry

    lax.fori_loop(0, span // BLOCK, body, 0, unroll=unroll)


def _attn_group(group, q, k, v, n_seq, seq_len, casts, others=()):
    dil = GROUPS[group][1]
    rows = q.shape[1]
    span = SPAN if dil > 1 else SPAN_NO_DILATION
    spans_per_seq = seq_len // span
    steps = n_seq * spans_per_seq
    step = lambda b, n: b * spans_per_seq + n
    cast_in_specs, cast_out_specs, cast_shapes = [], [], []
    for w, layer in casts:
        blk = (w.shape[1] // steps, w.shape[2])
        cast_in_specs.append(pl.BlockSpec((None,) + blk, lambda b, n, layer=layer: (layer, step(b, n), 0)))
        cast_out_specs.append(pl.BlockSpec(blk, lambda b, n: (step(b, n), 0)))
        cast_shapes.append(jax.ShapeDtypeStruct(w.shape[1:], BF16))
    cur_spec = pl.BlockSpec((SLABS_PER_GROUP, span, V7X_LANES), lambda b, n: (group, step(b, n), 0))
    slab_spec = pl.BlockSpec((SLABS_PER_GROUP, span, V7X_LANES), lambda b, n: (0, step(b, n), 0))
    if others:
        res_shapes = [jax.ShapeDtypeStruct((rows, GROUP_WIDTH), BF16)]
        res_specs = [pl.BlockSpec((span, GROUP_WIDTH), lambda b, n: (step(b, n), 0))]
    else:
        res_shapes = [jax.ShapeDtypeStruct((SLABS_PER_GROUP, rows, V7X_LANES), F32)] * 3
        res_specs = [slab_spec] * 3
    span_buf = (SLABS_PER_GROUP, span, V7X_LANES)
    scratch = [pltpu.VMEM((2,) + span_buf, F32), pltpu.VMEM((2,) + span_buf, F32),
               pltpu.VMEM((2, 2 * BLOCK, 2 * BLOCK), F32)]
    if dil > 1:
        scratch.append(pltpu.VMEM(span_buf, F32))
    if dil > STRIDE_STEP:
        scratch.append(pltpu.VMEM((span, V7X_LANES), F32))
    return pl.pallas_call(
        functools.partial(_attn_kernel, dil, ATTN_UNROLL[dil], span, len(casts), len(others)),
        out_shape=res_shapes + cast_shapes, grid=(n_seq, spans_per_seq),
        in_specs=[cur_spec, cur_spec, cur_spec] + [slab_spec] * len(others) + cast_in_specs,
        out_specs=res_specs + cast_out_specs,
        scratch_shapes=scratch,
        name="attn_prompt_d%d" % dil,
        compiler_params=pltpu.CompilerParams(
            dimension_semantics=("arbitrary", "arbitrary"), vmem_limit_bytes=V7X_VMEM_LIMIT),
    )(q, k, v, *others, *[w for w, _ in casts])


def _tail_kernel(parts, *refs):
    (h_ref, attn_ref, cbc_ref, sg_ref, ple_ref, wa_ref, wc_ref, wo_ref, wu_ref, wd_ref, wg_ref, wp_ref,
     gains_ref, out_ref) = refs

    pr = h_ref.shape[0] // parts
    row_slices = [slice(i * pr, (i + 1) * pr) for i in range(parts)]

    mixed = []
    for rows in row_slices:
        a = _dot(attn_ref[rows, :].astype(BF16), wa_ref[...])
        c = _dot(cbc_ref[rows, :], wc_ref[...])
        sg = sg_ref[rows, :].astype(F32)
        mix = sg[:, 0:D_MODEL] * a + sg[:, D_MODEL:2 * D_MODEL] * c
        mixed.append(_dot(mix.astype(BF16), wo_ref[...]))

    embeds = [_dot(ple_ref[rows, :].astype(BF16), wp_ref[...]) for rows in row_slices]

    for rows, mo in zip(row_slices, mixed):
        out_ref[rows, :] = h_ref[rows, :] + _rms(mo, gains_ref[1:2, :])

    mlp = []
    for rows in row_slices:
        xn = _rms(out_ref[rows, :], gains_ref[2:3, :]).astype(BF16)
        acc = None
        for lo in range(0, D_FF, MLP_CHUNK):
            f = _dot(xn, wu_ref[:, lo:lo + MLP_CHUNK])
            f = jnp.square(jnp.maximum(f, 0.0)).astype(BF16)
            t = _dot(f, wd_ref[lo:lo + MLP_CHUNK, :])
            acc = t if acc is None else acc + t
        mlp.append(acc)

    for rows, acc in zip(row_slices, mlp):
        out_ref[rows, :] = out_ref[rows, :] + _rms(acc, gains_ref[3:4, :])

    for rows, embed in zip(row_slices, embeds):
        h = out_ref[rows, :]
        gate = jax.nn.sigmoid(_dot(_rms(h, gains_ref[4:5, :]).astype(BF16), wg_ref[...]))
        out_ref[rows, :] = h + gate * embed


def _tail(layer, h, attn, cbc, sg, ple, weights, gains, *, tm, parts):
    rows = h.shape[0]
    row = lambda i: (i, 0)
    in_specs = [
        pl.BlockSpec((tm, D_MODEL), row),
        pl.BlockSpec((tm, GROUP_WIDTH), row),
        pl.BlockSpec((tm, CONV_CHANNELS), row),
        pl.BlockSpec((tm, 2 * D_MODEL), row),
        pl.BlockSpec((None, tm, PLE_DIM), lambda i: (layer, i, 0)),
    ]
    in_specs += [_weight_spec(w, layer) for w in weights]
    in_specs += [pl.BlockSpec((None,) + gains.shape[1:], lambda i: (layer, 0, 0))]
    return pl.pallas_call(
        functools.partial(_tail_kernel, parts),
        out_shape=jax.ShapeDtypeStruct((rows, D_MODEL), F32),
        grid=(rows // tm,), in_specs=in_specs,
        out_specs=pl.BlockSpec((tm, D_MODEL), row),
        name="tail_prompt" if rows > tm else "tail_sample",
        compiler_params=pltpu.CompilerParams(
            dimension_semantics=("arbitrary",), vmem_limit_bytes=V7X_VMEM_LIMIT),
    )(h, attn, cbc, sg, ple, *weights, gains)


def _shift_in(old, new_col):
    width = old.shape[1]
    tiles = width // V7X_LANES
    lane = lax.broadcasted_iota(jnp.int32, (old.shape[0], V7X_LANES), 1)
    last = lane == V7X_LANES - 1
    rot = [pltpu.roll(old[:, t * V7X_LANES:(t + 1) * V7X_LANES], V7X_LANES - 1, 1)
           for t in range(tiles)]
    out = [jnp.where(last, rot[t + 1] if t + 1 < tiles else new_col, rot[t]) for t in range(tiles)]
    return out[0] if tiles == 1 else jnp.concatenate(out, axis=1)


def _own_head_mask():
    head_row = lax.broadcasted_iota(jnp.int32, (V7X_SUBLANES, GROUP_WIDTH), 0)
    head_lane = lax.broadcasted_iota(jnp.int32, (V7X_SUBLANES, GROUP_WIDTH), 1) // HEAD_DIM
    return head_row == head_lane


def _new_column(row):
    return jnp.transpose(jnp.broadcast_to(row, (V7X_LANES, GROUP_WIDTH)))


def _values_out(p, p_new, vt, v_new):
    return lax.dot_general(p.astype(BF16), vt.astype(BF16), (((1,), (1,)), ((), ())),
                           preferred_element_type=F32) + p_new * v_new


def _sample_keys(q_ref, kn_ref, vn_ref, caches, outs, part_o_ref, part_p_ref, part_s_ref):
    q = q_ref[...]
    kn = kn_ref[...]
    vn = vn_ref[...]
    own_head = _own_head_mask()
    stats = []
    for g, (window, dil) in enumerate(GROUPS):
        sl = slice(g * GROUP_WIDTH, (g + 1) * GROUP_WIDTH)
        kt = caches[2 * g][...].reshape(GROUP_WIDTH, window)
        q_heads = jnp.where(own_head, jnp.broadcast_to(q[:, sl], own_head.shape), 0.0)
        s = _dot(q_heads.astype(BF16), kt.astype(BF16))
        pos = lax.broadcasted_iota(jnp.int32, s.shape, 1)
        s = jnp.where(pos % dil == 0, s, NEG_INF)
        s_new = jnp.sum(q_heads * kn[:, sl], axis=-1, keepdims=True)
        m = jnp.maximum(jnp.max(s, axis=-1, keepdims=True), s_new)
        p = jnp.exp(s - m)
        p_new = jnp.exp(s_new - m)
        l = jnp.sum(p, axis=-1, keepdims=True) + p_new
        stats += [m, l]
        outs[2 * g][...] = _shift_in(kt, _new_column(kn[:, sl])).reshape(HEADS_PER_GROUP, HEAD_DIM, window)
        if g + 1 < N_GROUPS:
            vt = caches[2 * g + 1][...].reshape(GROUP_WIDTH, window)
            part_o_ref[g] = _values_out(p, p_new, vt, vn[:, sl]) / l
            outs[2 * g + 1][...] = _shift_in(vt, _new_column(vn[:, sl])).reshape(
                HEADS_PER_GROUP, HEAD_DIM, window)
        else:
            part_p_ref[...] = p
            stats.append(p_new)
    lane = lax.broadcasted_iota(jnp.int32, (V7X_SUBLANES, V7X_LANES), 1)
    packed = jnp.zeros((V7X_SUBLANES, V7X_LANES), F32)
    for j, col in enumerate(stats):
        packed = jnp.where(lane == j, col, packed)
    part_s_ref[...] = packed


def _sample_values_kernel(n_alias, vn_ref, part_o_ref, part_p_ref, part_s_ref, cv_ref, *rest):
    attn_ref, nv_ref = rest[n_alias], rest[n_alias + 1]
    g = N_GROUPS - 1
    window = GROUPS[g][0]
    for b in range(vn_ref.shape[0]):
        stats = part_s_ref[b]
        col = lambda j, stats=stats: stats[:, j:j + 1]
        v_new = vn_ref[b][:, g * GROUP_WIDTH:(g + 1) * GROUP_WIDTH]
        vt = cv_ref[b].reshape(GROUP_WIDTH, window)
        parts = [(part_o_ref[b, j], col(2 * j), col(2 * j + 1)) for j in range(g)]
        parts.append((_values_out(part_p_ref[b], col(2 * N_GROUPS), vt, v_new) / col(2 * g + 1),
                      col(2 * g), col(2 * g + 1)))
        mx = jnp.maximum(jnp.maximum(parts[0][1], parts[1][1]), parts[2][1])
        ws = [l * jnp.exp(m - mx) for _, m, l in parts]
        num = ws[0] * parts[0][0] + ws[1] * parts[1][0] + ws[2] * parts[2][0]
        merged = num / (ws[0] + ws[1] + ws[2])
        attn_ref[b] = jnp.sum(jnp.where(_own_head_mask(), merged, 0.0), axis=0, keepdims=True)
        nv_ref[b] = _shift_in(vt, _new_column(v_new)).reshape(HEADS_PER_GROUP, HEAD_DIM, window)


def _sample_values(layer, v_new, part_o, part_p, part_s, cache_v, new_cache_v):
    batch = v_new.shape[0]
    bb = SAMPLE_VALUES_BATCH
    per_seq = lambda a: pl.BlockSpec((bb,) + a.shape[1:], lambda b: (b,) + (0,) * (a.ndim - 1))
    cache_spec = pl.BlockSpec((None, bb) + cache_v.shape[2:], lambda b: (layer, b, 0, 0, 0))
    in_specs = [per_seq(v_new), per_seq(part_o), per_seq(part_p), per_seq(part_s), cache_spec]
    args = [v_new, part_o, part_p, part_s, cache_v]
    aliases = {}
    n_alias = 0
    if new_cache_v is not None:
        n_alias = 1
        in_specs.append(pl.BlockSpec(memory_space=pl.ANY))
        aliases = {len(args): 1}
        args.append(new_cache_v)
    return pl.pallas_call(
        functools.partial(_sample_values_kernel, n_alias),
        out_shape=[jax.ShapeDtypeStruct((batch, 1, GROUP_WIDTH), F32),
                   jax.ShapeDtypeStruct(cache_v.shape, F32)],
        grid=(batch // bb,), in_specs=in_specs,
        out_specs=[pl.BlockSpec((bb, 1, GROUP_WIDTH), lambda b: (b, 0, 0)), cache_spec],
        input_output_aliases=aliases, name="attn_sample_values",
        compiler_params=pltpu.CompilerParams(
            dimension_semantics=("arbitrary",), vmem_limit_bytes=V7X_VMEM_LIMIT),
    )(*args)


def _rope_tables(pos):
    half = HEAD_DIM // 2
    inv = np.power(ROPE_THETA, -2.0 * np.arange(half, dtype=np.float64) / HEAD_DIM)
    ang = np.asarray(pos, dtype=np.float64)[:, None] * inv[None, :]
    cos, sin = np.cos(ang), np.sin(ang)
    reps = V7X_LANES // HEAD_DIM
    return (np.concatenate([cos, cos] * reps, axis=1).astype(np.float32),
            np.concatenate([-sin, sin] * reps, axis=1).astype(np.float32))


def kernel(x_prompt, x_sample, p_prompt, p_sample, cache_k_w128, cache_v_w128, cache_k_w512, cache_v_w512, cache_k_w2048, cache_v_w2048, state_conv, w_in, conv_w, w_attn_out, w_conv_out, w_o, g_pre_mix, g_post_mix, w_up, w_down, g_pre_mlp, g_post_mlp, g_ple, w_ple_gate, w_ple_proj):
    n_seq, seq_len, _ = x_prompt.shape
    n_dec = x_sample.shape[0]
    rows = n_seq * seq_len

    w_in_b = w_in[0].astype(BF16)
    w_attn_out_b = w_attn_out.astype(BF16)
    w_ple_proj_b = w_ple_proj.astype(BF16)
    gains = jnp.stack([g_pre_mix, g_post_mix, g_pre_mlp, g_post_mlp, g_ple], axis=1)
    g_in = g_pre_mix[:, None, :]

    cs_p, sn_p = _rope_tables(np.arange(seq_len))
    cs_s, sn_s = _rope_tables(np.full((n_dec,), PAST_LEN))

    caches = (cache_k_w128, cache_v_w128, cache_k_w512, cache_v_w512, cache_k_w2048, cache_v_w2048)
    caches_t = [jnp.transpose(c, (0, 1, 3, 4, 2)) for c in caches]

    hp = x_prompt.reshape(rows, D_MODEL)
    hs = x_sample.reshape(n_dec, D_MODEL)
    ple_p = p_prompt.reshape(DEPTH, rows, PLE_DIM)
    ple_s = p_sample.reshape(DEPTH, n_dec, PLE_DIM)

    conv_p, conv_s = [], []
    states_p = None
    key_caches = None
    value_cache = None
    n_states = 2 * N_GROUPS
    as_rows = lambda t: jnp.transpose(t, (1, 0, 2)).reshape(n_dec, 1, ATTN_WIDTH)
    for i in range(DEPTH):
        prev = (state_conv[i, :, 0, :], state_conv[i, :, 1, :])
        qs, ks, vs, cbcs, sgs, us = _inproj(i, hs, cs_s, sn_s, g_in, w_in_b, conv_w, prev=prev)
        vs_rows = as_rows(vs)
        side = (as_rows(qs), as_rows(ks), vs_rows, caches_t[:N_KEY_SIDE_CACHES], key_caches)
        res = _inproj(i, hp, cs_p, sn_p, g_in, w_in_b, conv_w, seq_len=seq_len, states=states_p,
                      side=side)
        q, k, v, cbc, sg, cst = res[:6]
        states_p = res[6:6 + n_states]
        key_caches = res[6 + n_states:6 + n_states + N_KEY_SIDE_CACHES]
        part_o, part_p, part_s = res[6 + n_states + N_KEY_SIDE_CACHES:]
        casts = ([(w_up, i), (w_o, i), (w_ple_gate, i), (w_conv_out, i)], [(w_down, i)],
                 [(w_in, i + 1)] if i + 1 < DEPTH else [])
        res16 = _attn_group(2, q, k, v, n_seq, seq_len, casts[2])
        res4 = _attn_group(1, q, k, v, n_seq, seq_len, casts[1])
        res1 = _attn_group(0, q, k, v, n_seq, seq_len, casts[0], others=res4[:3] + res16[:3])
        attn_p, w_up_b, w_o_b, w_ple_gate_b, w_conv_out_b = res1
        w_down_b = res4[3]
        w_in_next = res16[3] if len(res16) > 3 else None
        tail_w = [w_attn_out_b, w_conv_out_b, w_o_b, w_up_b, w_down_b, w_ple_gate_b, w_ple_proj_b]
        hp = _tail(i, hp, attn_p, cbc, sg, ple_p, tail_w, gains, tm=TAIL_ROW_TILE, parts=ROW_PARTS)
        conv_p.append(cst[:, V7X_SUBLANES - (CONV_WIDTH - 1):, :])

        attn_s, value_cache = _sample_values(i, vs_rows, part_o, part_p, part_s, caches_t[-1], value_cache)
        hs = _tail(i, hs, attn_s.reshape(n_dec, GROUP_WIDTH), cbcs, sgs, ple_s, tail_w, gains,
                   tm=n_dec, parts=1)
        conv_s.append(jnp.stack([prev[1], us], axis=1))
        w_in_b = w_in_next

    to_rows = lambda c: jnp.transpose(c, (0, 1, 4, 2, 3))
    return (hp.reshape(n_seq, seq_len, D_MODEL), hs.reshape(n_dec, 1, D_MODEL),
            *[to_rows(c) for c in states_p], jnp.stack(conv_p),
            *[to_rows(c) for c in (*key_caches, value_cache)], jnp.stack(conv_s))
```

```python
import functools

import jax
import jax.numpy as jnp
import numpy as np
from jax import lax
from jax.experimental import pallas as pl
from jax.experimental.pallas import tpu as pltpu

F32 = jnp.float32
BF16 = jnp.bfloat16

D_MODEL = 1024
DEPTH = 4
PAST_LEN = 8192
HEAD_DIM = 64
HEADS_PER_GROUP = 4
GROUPS = ((128, 1), (512, 4), (2048, 16))
N_GROUPS = len(GROUPS)
GROUP_WIDTH = HEADS_PER_GROUP * HEAD_DIM
ATTN_WIDTH = N_GROUPS * GROUP_WIDTH
CONV_CHANNELS = 768
CONV_WIDTH = 3
D_FF = 4 * D_MODEL
PLE_DIM = 256
ROPE_THETA = 10000.0
BLOCK = 128
RMS_EPS = 1e-6
NEG_INF = -1e30
IN_PROJ_WIDTH = 3 * ATTN_WIDTH + 3 * CONV_CHANNELS + 2 * D_MODEL
OFF_Q, OFF_K, OFF_V = 0, ATTN_WIDTH, 2 * ATTN_WIDTH
OFF_CB = 3 * ATTN_WIDTH
OFF_CC = OFF_CB + CONV_CHANNELS
OFF_CH = OFF_CC + CONV_CHANNELS
OFF_GA = OFF_CH + CONV_CHANNELS
OFF_GB = OFF_GA + D_MODEL

V7X_LANES = 128
V7X_SUBLANES = 8
V7X_VMEM_LIMIT = 60 * 1024 * 1024
SLABS = ATTN_WIDTH // V7X_LANES
SLABS_PER_GROUP = GROUP_WIDTH // V7X_LANES
HEADS_PER_SLAB = V7X_LANES // HEAD_DIM
ROW_TILE = 512
TAIL_ROW_TILE = 512
SPAN = BLOCK * GROUPS[-1][1]
SPAN_NO_DILATION = 1024
MLP_CHUNK = 1024
ROW_PARTS = 2
SAMPLE_VALUES_BATCH = 4
STRIDE_STEP = 4
ATTN_UNROLL = {1: 8, 4: 8, 16: 8}


def _rms(x, g):
    return x * lax.rsqrt(jnp.mean(x * x, axis=-1, keepdims=True) + RMS_EPS) * g


def _dot(a, b):
    return jnp.dot(a, b, preferred_element_type=F32)


def _resident(shape, index):
    return pl.BlockSpec(shape, index, pipeline_mode=pl.Buffered(1))


def _weight_spec(w, layer):
    if w.ndim == 2:
        return _resident(w.shape, lambda *_: (0, 0))
    return _resident((None,) + w.shape[1:], lambda *_: (layer, 0, 0))


N_KEY_SIDE_CACHES = 2 * N_GROUPS - 1
N_KEY_SIDE_IN = 3 + N_KEY_SIDE_CACHES
N_KEY_SIDE_OUT = N_KEY_SIDE_CACHES + 3


def _inproj_kernel(sample, tm, tiles_per_seq, n_alias, *refs):
    if sample:
        (x_ref, cs_ref, sn_ref, g_ref, w_ref, cw_ref, p0_ref, p1_ref,
         q_ref, k_ref, v_ref, cbc_ref, sg_ref, u_ref) = refs
    else:
        x_ref, cs_ref, sn_ref, g_ref, w_ref, cw_ref = refs[:6]
        side_in = refs[6:6 + N_KEY_SIDE_IN]
        n_in = 6 + N_KEY_SIDE_IN + n_alias
        q_ref, k_ref, v_ref, cbc_ref, sg_ref, cst_ref = refs[n_in:n_in + 6]
        state_refs = refs[n_in + 6:n_in + 6 + 2 * N_GROUPS]
        side_out = refs[n_in + 6 + 2 * N_GROUPS:n_in + 6 + 2 * N_GROUPS + N_KEY_SIDE_OUT]
        u_scr = refs[-1]

    parts = 1 if sample else ROW_PARTS
    pr = tm // parts
    halo = V7X_SUBLANES
    lane = lax.broadcasted_iota(jnp.int32, (pr, V7X_LANES), 1)
    lower = (lane % HEAD_DIM) < (HEAD_DIM // 2)
    cw = cw_ref[...]

    if not sample:
        @pl.when(pl.program_id(0) % tiles_per_seq == 0)
        def _():
            u_scr[0:halo, :] = jnp.zeros((halo, CONV_CHANNELS), F32)

        _sample_keys(side_in[0], side_in[1], side_in[2], side_in[3:], side_out[:N_KEY_SIDE_CACHES],
                     *side_out[N_KEY_SIDE_CACHES:])

    xns, cbs = [], []
    for i in range(parts):
        rows = slice(i * pr, (i + 1) * pr)
        xn = _rms(x_ref[rows, :], g_ref[...]).astype(BF16)
        xns.append(xn)
        cs = cs_ref[rows, :]
        sn = sn_ref[rows, :]

        def proj(lo, width, xn=xn):
            return _dot(xn, w_ref[:, lo:lo + width])

        def rope_store(t, out_ref, scale, rows=rows, cs=cs, sn=sn):
            for c in range(SLABS):
                xc = t[:, c * V7X_LANES:(c + 1) * V7X_LANES]
                sw = jnp.where(lower, pltpu.roll(xc, V7X_LANES - HEAD_DIM // 2, 1),
                               pltpu.roll(xc, HEAD_DIM // 2, 1))
                r = xc * cs + sw * sn
                if scale is not None:
                    r = r * scale
                out_ref[c, rows, :] = r

        rope_store(proj(OFF_Q, ATTN_WIDTH), q_ref, HEAD_DIM ** -0.5)
        rope_store(proj(OFF_K, ATTN_WIDTH), k_ref, None)
        vv = proj(OFF_V, ATTN_WIDTH)
        for c in range(SLABS):
            v_ref[c, rows, :] = vv[:, c * V7X_LANES:(c + 1) * V7X_LANES]
        cbs.append(proj(OFF_CB, CONV_CHANNELS))
        u = proj(OFF_CC, CONV_CHANNELS) * proj(OFF_CH, CONV_CHANNELS)
        if sample:
            u_ref[...] = u
        else:
            u_scr[halo + i * pr:halo + (i + 1) * pr, :] = u

    for i in range(parts):
        rows = slice(i * pr, (i + 1) * pr)
        if sample:
            conv = cw[0:1] * p0_ref[...] + cw[1:2] * p1_ref[...] + cw[2:3] * u_ref[...]
        else:
            lo = halo + i * pr
            conv = (cw[0:1] * u_scr[lo - 2:lo + pr - 2, :] + cw[1:2] * u_scr[lo - 1:lo + pr - 1, :]
                    + cw[2:3] * u_scr[lo:lo + pr, :])
        cbc_ref[rows, :] = (cbs[i] * conv).astype(BF16)
        for lo, off in ((0, OFF_GA), (D_MODEL, OFF_GB)):
            gate = jax.nn.sigmoid(_dot(xns[i], w_ref[:, off:off + D_MODEL]))
            sg_ref[rows, lo:lo + D_MODEL] = gate.astype(BF16)

    if not sample:
        tail = u_scr[tm:tm + halo, :]
        u_scr[0:halo, :] = tail
        cst_ref[...] = tail
        for g, (window, _) in enumerate(GROUPS):
            keep = min(window, tm)
            for j, src in enumerate((k_ref, v_ref)):
                for half in range(SLABS_PER_GROUP):
                    last = src[g * SLABS_PER_GROUP + half, tm - keep:tm, :]
                    heads = slice(half * HEADS_PER_SLAB, (half + 1) * HEADS_PER_SLAB)
                    state_refs[2 * g + j][heads, :, :] = jnp.transpose(last).reshape(
                        HEADS_PER_SLAB, HEAD_DIM, keep)


def _inproj(layer, x, cs, sn, gains, w_in, conv_w, *, seq_len=None, prev=None, states=None, side=None):
    rows = x.shape[0]
    sample = prev is not None
    tm = rows if sample else ROW_TILE
    tiles_per_seq = 1 if sample else seq_len // tm
    grid = (rows // tm,)
    row = lambda i: (i, 0)
    slab_spec = pl.BlockSpec((SLABS, tm, V7X_LANES), lambda i: (0, i, 0))
    table_spec = pl.BlockSpec((tm, V7X_LANES), lambda i: (i % tiles_per_seq, 0))
    in_specs = [
        pl.BlockSpec((tm, D_MODEL), row),
        table_spec, table_spec,
        pl.BlockSpec((None, 1, D_MODEL), lambda i: (layer, 0, 0)),
        _weight_spec(w_in, layer),
        pl.BlockSpec((None, CONV_WIDTH, CONV_CHANNELS), lambda i: (layer, 0, 0)),
    ]
    args = [x, cs, sn, gains, w_in, conv_w]
    slab_shape = jax.ShapeDtypeStruct((SLABS, rows, V7X_LANES), F32)
    out_shape = [slab_shape, slab_shape, slab_shape,
                 jax.ShapeDtypeStruct((rows, CONV_CHANNELS), BF16),
                 jax.ShapeDtypeStruct((rows, 2 * D_MODEL), BF16)]
    out_specs = [slab_spec, slab_spec, slab_spec,
                 pl.BlockSpec((tm, CONV_CHANNELS), row),
                 pl.BlockSpec((tm, 2 * D_MODEL), row)]
    scratch = []
    aliases = {}
    n_alias = 0
    if sample:
        in_specs += [pl.BlockSpec((tm, CONV_CHANNELS), row)] * 2
        args += list(prev)
        out_shape.append(jax.ShapeDtypeStruct((rows, CONV_CHANNELS), F32))
        out_specs.append(pl.BlockSpec((tm, CONV_CHANNELS), row))
    else:
        n_seq = rows // seq_len
        out_shape.append(jax.ShapeDtypeStruct((n_seq, V7X_SUBLANES, CONV_CHANNELS), F32))
        out_specs.append(pl.BlockSpec((None, V7X_SUBLANES, CONV_CHANNELS),
                                      lambda i: (i // tiles_per_seq, 0, 0)))
        sq, sk, sv, side_caches, side_new = side
        n_dec = sq.shape[0]
        assert n_dec == grid[0] and len(side_caches) == N_KEY_SIDE_CACHES
        vec_spec = pl.BlockSpec((None, 1, ATTN_WIDTH), lambda i: (i, 0, 0))
        cache_specs = [pl.BlockSpec((None, None) + c.shape[2:], lambda i: (layer, i, 0, 0, 0))
                       for c in side_caches]
        in_specs += [vec_spec] * 3 + cache_specs
        args += [sq, sk, sv, *side_caches]
        if states is not None:
            carried = list(states) + list(side_new)
            n_alias = len(carried)
            in_specs += [pl.BlockSpec(memory_space=pl.ANY)] * n_alias
            aliases = {len(args) + j: len(out_shape) + j for j in range(n_alias)}
            args += carried
        for window, _ in GROUPS:
            keep = min(window, seq_len)
            blk = min(keep, tm)
            first_tile = tiles_per_seq - keep // blk

            def state_index(i, first_tile=first_tile):
                return (layer, i // tiles_per_seq, 0, 0,
                        jnp.maximum(i % tiles_per_seq - first_tile, 0))

            for _ in range(2):
                out_shape.append(jax.ShapeDtypeStruct(
                    (DEPTH, n_seq, HEADS_PER_GROUP, HEAD_DIM, keep), F32))
                out_specs.append(pl.BlockSpec((None, None, HEADS_PER_GROUP, HEAD_DIM, blk), state_index))
        out_shape += [jax.ShapeDtypeStruct(c.shape, F32) for c in side_caches]
        out_specs += cache_specs
        widest = GROUPS[-1][0]
        for shape in ((N_GROUPS - 1, V7X_SUBLANES, GROUP_WIDTH), (V7X_SUBLANES, widest),
                      (V7X_SUBLANES, V7X_LANES)):
            out_shape.append(jax.ShapeDtypeStruct((n_dec,) + shape, F32))
            out_specs.append(pl.BlockSpec((None,) + shape, lambda i, nd=len(shape): (i,) + (0,) * nd))
        scratch.append(pltpu.VMEM((tm + V7X_SUBLANES, CONV_CHANNELS), F32))
    return pl.pallas_call(
        functools.partial(_inproj_kernel, sample, tm, tiles_per_seq, n_alias),
        out_shape=out_shape, grid=grid, in_specs=in_specs, out_specs=out_specs,
        scratch_shapes=scratch, input_output_aliases=aliases,
        name="inproj_sample" if sample else "inproj_prompt",
        compiler_params=pltpu.CompilerParams(
            dimension_semantics=("arbitrary",), vmem_limit_bytes=V7X_VMEM_LIMIT),
    )(*args)


def _deinterleave(src, dst, tmp, dil, span):
    assert dil in (1, STRIDE_STEP, STRIDE_STEP * STRIDE_STEP)
    for c in range(SLABS_PER_GROUP):
        if dil == 1:
            dst[c] = src[c]
            continue
        seg = span // STRIDE_STEP
        stage = dst.at[c] if dil == STRIDE_STEP else tmp
        for r in range(STRIDE_STEP):
            stage[r * seg:(r + 1) * seg, :] = src[c, pl.ds(r, seg, stride=STRIDE_STEP), :]
        if dil == STRIDE_STEP:
            continue
        seg2 = seg // STRIDE_STEP
        for r in range(STRIDE_STEP):
            for r2 in range(STRIDE_STEP):
                res = r + STRIDE_STEP * r2
                dst[c, res * seg2:(res + 1) * seg2, :] = tmp[pl.ds(r * seg + r2, seg2, stride=STRIDE_STEP), :]


def _attn_kernel(dil, unroll, span, n_cast, n_other, *refs):
    q_ref, k_ref, v_ref = refs[:3]
    others = refs[3:3 + n_other]
    cast_in = refs[3 + n_other:3 + n_other + n_cast]
    n_in = 3 + n_other + n_cast
    n_res = 1 if n_other else 2
    res_refs = refs[n_in:n_in + n_res]
    cast_out = refs[n_in + n_res:n_in + n_res + n_cast]
    scratch = list(refs[n_in + n_res + n_cast:])
    kd, vd, bias = scratch[:3]
    qd = scratch[3] if dil > 1 else None
    tmp = scratch[4] if dil > STRIDE_STEP else None
    assert not (n_other and dil > 1)

    for src, dst in zip(cast_in, cast_out):
        dst[...] = src[...].astype(BF16)

    seg = span // dil
    span_idx = pl.program_id(1)
    par = span_idx % 2
    first_span = span_idx == 0

    qi = lax.broadcasted_iota(jnp.int32, (2 * BLOCK, 2 * BLOCK), 0) % BLOCK
    kj = lax.broadcasted_iota(jnp.int32, (2 * BLOCK, 2 * BLOCK), 1)
    own = kj >= BLOCK
    slack = jnp.where(own, qi - (kj - BLOCK), kj - qi)
    bias[0] = jnp.where(slack >= 0, 0.0, NEG_INF)
    bias[1] = jnp.where(jnp.where(own, slack, -1) >= 0, 0.0, NEG_INF)
    low_q = lax.broadcasted_iota(jnp.int32, (BLOCK, V7X_LANES), 1) < HEAD_DIM
    low_kv = lax.broadcasted_iota(jnp.int32, (2 * BLOCK, V7X_LANES), 1) < HEAD_DIM

    @pl.when(first_span)
    def _():
        kd[1] = jnp.zeros(kd.shape[1:], F32)
        vd[1] = jnp.zeros(vd.shape[1:], F32)

    q_rows = q_ref
    if dil > 1:
        _deinterleave(q_ref, qd, tmp, dil, span)
        q_rows = qd
    _deinterleave(k_ref, kd.at[par], tmp, dil, span)
    _deinterleave(v_ref, vd.at[par], tmp, dil, span)

    def body(blk, carry):
        res = blk % dil
        sub = blk // dil
        base = pl.multiple_of(res * seg + sub * BLOCK, BLOCK)
        inner = sub > 0
        pbuf = jnp.where(inner, par, 1 - par)
        prow = pl.multiple_of(jnp.where(inner, base - BLOCK, res * seg + seg - BLOCK), BLOCK)
        no_prev = jnp.logical_and(first_span, sub == 0).astype(jnp.int32)
        row0 = res + dil * BLOCK * sub
        out_rows = pl.ds(row0, BLOCK, stride=dil) if dil > 1 else pl.ds(base, BLOCK)
        for c in range(SLABS_PER_GROUP):
            q = q_rows[c, pl.ds(base, BLOCK), :]
            k2 = jnp.concatenate([kd[pbuf, c, pl.ds(prow, BLOCK), :],
                                  kd[par, c, pl.ds(base, BLOCK), :]], axis=0)
            v2 = jnp.concatenate([vd[pbuf, c, pl.ds(prow, BLOCK), :],
                                  vd[par, c, pl.ds(base, BLOCK), :]], axis=0)
            q2 = jnp.concatenate([jnp.where(low_q, q, 0.0), jnp.where(low_q, 0.0, q)], axis=0)
            s = lax.dot_general(q2.astype(BF16), k2.astype(BF16), (((1,), (1,)), ((), ())),
                                preferred_element_type=F32)
            s = s + bias[no_prev]
            m = jnp.max(s, axis=-1, keepdims=True)
            p = jnp.exp(s - m)
            l = jnp.sum(p, axis=-1, keepdims=True)
            pb = p.astype(BF16)
            p_cat = jnp.concatenate([pb[0:BLOCK], pb[BLOCK:2 * BLOCK]], axis=1)
            v_stack = jnp.concatenate([jnp.where(low_kv, v2, 0.0), jnp.where(low_kv, 0.0, v2)],
                                      axis=0).astype(BF16)
            o = _dot(p_cat, v_stack)
            l_slab = jnp.where(low_q, l[0:BLOCK], l[BLOCK:2 * BLOCK])
            m_slab = jnp.where(low_q, m[0:BLOCK], m[BLOCK:2 * BLOCK])
            o_norm = o / l_slab
            lse = m_slab + jnp.log(l_slab)
            if not n_other:
                o_ref, lse_ref = res_refs
                o_ref[c, out_rows, :] = o_norm
                lse_ref[c, out_rows, :] = lse
                continue
            outs = [o_norm] + [others[j][c, out_rows, :] for j in range(0, n_other, 2)]
            lses = [lse] + [others[j][c, out_rows, :] for j in range(1, n_other, 2)]
            top = functools.reduce(jnp.maximum, lses)
            ws = [jnp.exp(x - top) for x in lses]
            num = functools.reduce(lambda x, y: x + y, [w * og for w, og in zip(ws, outs)])
            den = functools.reduce(lambda x, y: x + y, ws)
            res_refs[0][out_rows, c * V7X_LANES:(c + 1) * V7X_LANES] = (num / den).astype(BF16)
        return carry

    lax.fori_loop(0, span // BLOCK, body, 0, unroll=unroll)


def _attn_group(group, q, k, v, n_seq, seq_len, casts, others=()):
    dil = GROUPS[group][1]
    rows = q.shape[1]
    span = SPAN if dil > 1 else SPAN_NO_DILATION
    spans_per_seq = seq_len // span
    steps = n_seq * spans_per_seq
    step = lambda b, n: b * spans_per_seq + n
    cast_in_specs, cast_out_specs, cast_shapes = [], [], []
    for w, layer in casts:
        blk = (w.shape[1] // steps, w.shape[2])
        cast_in_specs.append(pl.BlockSpec((None,) + blk, lambda b, n, layer=layer: (layer, step(b, n), 0)))
        cast_out_specs.append(pl.BlockSpec(blk, lambda b, n: (step(b, n), 0)))
        cast_shapes.append(jax.ShapeDtypeStruct(w.shape[1:], BF16))
    cur_spec = pl.BlockSpec((SLABS_PER_GROUP, span, V7X_LANES), lambda b, n: (group, step(b, n), 0))
    slab_spec = pl.BlockSpec((SLABS_PER_GROUP, span, V7X_LANES), lambda b, n: (0, step(b, n), 0))
    if others:
        res_shapes = [jax.ShapeDtypeStruct((rows, GROUP_WIDTH), BF16)]
        res_specs = [pl.BlockSpec((span, GROUP_WIDTH), lambda b, n: (step(b, n), 0))]
    else:
        res_shapes = [jax.ShapeDtypeStruct((SLABS_PER_GROUP, rows, V7X_LANES), F32)] * 2
        res_specs = [slab_spec] * 2
    span_buf = (SLABS_PER_GROUP, span, V7X_LANES)
    scratch = [pltpu.VMEM((2,) + span_buf, F32), pltpu.VMEM((2,) + span_buf, F32),
               pltpu.VMEM((2, 2 * BLOCK, 2 * BLOCK), F32)]
    if dil > 1:
        scratch.append(pltpu.VMEM(span_buf, F32))
    if dil > STRIDE_STEP:
        scratch.append(pltpu.VMEM((span, V7X_LANES), F32))
    return pl.pallas_call(
        functools.partial(_attn_kernel, dil, ATTN_UNROLL[dil], span, len(casts), len(others)),
        out_shape=res_shapes + cast_shapes, grid=(n_seq, spans_per_seq),
        in_specs=[cur_spec, cur_spec, cur_spec] + [slab_spec] * len(others) + cast_in_specs,
        out_specs=res_specs + cast_out_specs,
        scratch_shapes=scratch,
        name="attn_prompt_d%d" % dil,
        compiler_params=pltpu.CompilerParams(
            dimension_semantics=("arbitrary", "arbitrary"), vmem_limit_bytes=V7X_VMEM_LIMIT),
    )(q, k, v, *others, *[w for w, _ in casts])


def _tail_kernel(parts, *refs):
    (h_ref, attn_ref, cbc_ref, sg_ref, ple_ref, wa_ref, wc_ref, wo_ref, wu_ref, wd_ref, wg_ref, wp_ref,
     gains_ref, out_ref) = refs

    pr = h_ref.shape[0] // parts
    row_slices = [slice(i * pr, (i + 1) * pr) for i in range(parts)]

    mixed = []
    for rows in row_slices:
        a = _dot(attn_ref[rows, :].astype(BF16), wa_ref[...])
        c = _dot(cbc_ref[rows, :], wc_ref[...])
        sg = sg_ref[rows, :].astype(F32)
        mix = sg[:, 0:D_MODEL] * a + sg[:, D_MODEL:2 * D_MODEL] * c
        mixed.append(_dot(mix.astype(BF16), wo_ref[...]))

    embeds = [_dot(ple_ref[rows, :].astype(BF16), wp_ref[...]) for rows in row_slices]

    for rows, mo in zip(row_slices, mixed):
        out_ref[rows, :] = h_ref[rows, :] + _rms(mo, gains_ref[1:2, :])

    mlp = []
    for rows in row_slices:
        xn = _rms(out_ref[rows, :], gains_ref[2:3, :]).astype(BF16)
        acc = None
        for lo in range(0, D_FF, MLP_CHUNK):
            f = _dot(xn, wu_ref[:, lo:lo + MLP_CHUNK])
            f = jnp.square(jnp.maximum(f, 0.0)).astype(BF16)
            t = _dot(f, wd_ref[lo:lo + MLP_CHUNK, :])
            acc = t if acc is None else acc + t
        mlp.append(acc)

    for rows, acc in zip(row_slices, mlp):
        out_ref[rows, :] = out_ref[rows, :] + _rms(acc, gains_ref[3:4, :])

    for rows, embed in zip(row_slices, embeds):
        h = out_ref[rows, :]
        gate = jax.nn.sigmoid(_dot(_rms(h, gains_ref[4:5, :]).astype(BF16), wg_ref[...]))
        out_ref[rows, :] = h + gate * embed


def _tail(layer, h, attn, cbc, sg, ple, weights, gains, *, tm, parts):
    rows = h.shape[0]
    row = lambda i: (i, 0)
    in_specs = [
        pl.BlockSpec((tm, D_MODEL), row),
        pl.BlockSpec((tm, GROUP_WIDTH), row),
        pl.BlockSpec((tm, CONV_CHANNELS), row),
        pl.BlockSpec((tm, 2 * D_MODEL), row),
        pl.BlockSpec((None, tm, PLE_DIM), lambda i: (layer, i, 0)),
    ]
    in_specs += [_weight_spec(w, layer) for w in weights]
    in_specs += [pl.BlockSpec((None,) + gains.shape[1:], lambda i: (layer, 0, 0))]
    return pl.pallas_call(
        functools.partial(_tail_kernel, parts),
        out_shape=jax.ShapeDtypeStruct((rows, D_MODEL), F32),
        grid=(rows // tm,), in_specs=in_specs,
        out_specs=pl.BlockSpec((tm, D_MODEL), row),
        name="tail_prompt" if rows > tm else "tail_sample",
        compiler_params=pltpu.CompilerParams(
            dimension_semantics=("arbitrary",), vmem_limit_bytes=V7X_VMEM_LIMIT),
    )(h, attn, cbc, sg, ple, *weights, gains)


def _shift_in(old, new_col):
    width = old.shape[1]
    tiles = width // V7X_LANES
    lane = lax.broadcasted_iota(jnp.int32, (old.shape[0], V7X_LANES), 1)
    last = lane == V7X_LANES - 1
    rot = [pltpu.roll(old[:, t * V7X_LANES:(t + 1) * V7X_LANES], V7X_LANES - 1, 1)
           for t in range(tiles)]
    out = [jnp.where(last, rot[t + 1] if t + 1 < tiles else new_col, rot[t]) for t in range(tiles)]
    return out[0] if tiles == 1 else jnp.concatenate(out, axis=1)


def _own_head_mask():
    head_row = lax.broadcasted_iota(jnp.int32, (V7X_SUBLANES, GROUP_WIDTH), 0)
    head_lane = lax.broadcasted_iota(jnp.int32, (V7X_SUBLANES, GROUP_WIDTH), 1) // HEAD_DIM
    return head_row == head_lane


def _new_column(row):
    return jnp.transpose(jnp.broadcast_to(row, (V7X_LANES, GROUP_WIDTH)))


def _values_out(p, p_new, vt, v_new):
    return lax.dot_general(p.astype(BF16), vt.astype(BF16), (((1,), (1,)), ((), ())),
                           preferred_element_type=F32) + p_new * v_new


def _sample_keys(q_ref, kn_ref, vn_ref, caches, outs, part_o_ref, part_p_ref, part_s_ref):
    q = q_ref[...]
    kn = kn_ref[...]
    vn = vn_ref[...]
    own_head = _own_head_mask()
    stats = []
    for g, (window, dil) in enumerate(GROUPS):
        sl = slice(g * GROUP_WIDTH, (g + 1) * GROUP_WIDTH)
        kt = caches[2 * g][...].reshape(GROUP_WIDTH, window)
        q_heads = jnp.where(own_head, jnp.broadcast_to(q[:, sl], own_head.shape), 0.0)
        s = _dot(q_heads.astype(BF16), kt.astype(BF16))
        pos = lax.broadcasted_iota(jnp.int32, s.shape, 1)
        s = jnp.where(pos % dil == 0, s, NEG_INF)
        s_new = jnp.sum(q_heads * kn[:, sl], axis=-1, keepdims=True)
        m = jnp.maximum(jnp.max(s, axis=-1, keepdims=True), s_new)
        p = jnp.exp(s - m)
        p_new = jnp.exp(s_new - m)
        l = jnp.sum(p, axis=-1, keepdims=True) + p_new
        stats += [m, l]
        outs[2 * g][...] = _shift_in(kt, _new_column(kn[:, sl])).reshape(HEADS_PER_GROUP, HEAD_DIM, window)
        if g + 1 < N_GROUPS:
            vt = caches[2 * g + 1][...].reshape(GROUP_WIDTH, window)
            part_o_ref[g] = _values_out(p, p_new, vt, vn[:, sl]) / l
            outs[2 * g + 1][...] = _shift_in(vt, _new_column(vn[:, sl])).reshape(
                HEADS_PER_GROUP, HEAD_DIM, window)
        else:
            part_p_ref[...] = p
            stats.append(p_new)
    lane = lax.broadcasted_iota(jnp.int32, (V7X_SUBLANES, V7X_LANES), 1)
    packed = jnp.zeros((V7X_SUBLANES, V7X_LANES), F32)
    for j, col in enumerate(stats):
        packed = jnp.where(lane == j, col, packed)
    part_s_ref[...] = packed


def _sample_values_kernel(n_alias, vn_ref, part_o_ref, part_p_ref, part_s_ref, cv_ref, *rest):
    attn_ref, nv_ref = rest[n_alias], rest[n_alias + 1]
    g = N_GROUPS - 1
    window = GROUPS[g][0]
    for b in range(vn_ref.shape[0]):
        stats = part_s_ref[b]
        col = lambda j, stats=stats: stats[:, j:j + 1]
        v_new = vn_ref[b][:, g * GROUP_WIDTH:(g + 1) * GROUP_WIDTH]
        vt = cv_ref[b].reshape(GROUP_WIDTH, window)
        parts = [(part_o_ref[b, j], col(2 * j), col(2 * j + 1)) for j in range(g)]
        parts.append((_values_out(part_p_ref[b], col(2 * N_GROUPS), vt, v_new) / col(2 * g + 1),
                      col(2 * g), col(2 * g + 1)))
        mx = jnp.maximum(jnp.maximum(parts[0][1], parts[1][1]), parts[2][1])
        ws = [l * jnp.exp(m - mx) for _, m, l in parts]
        num = ws[0] * parts[0][0] + ws[1] * parts[1][0] + ws[2] * parts[2][0]
        merged = num / (ws[0] + ws[1] + ws[2])
        attn_ref[b] = jnp.sum(jnp.where(_own_head_mask(), merged, 0.0), axis=0, keepdims=True)
        nv_ref[b] = _shift_in(vt, _new_column(v_new)).reshape(HEADS_PER_GROUP, HEAD_DIM, window)


def _sample_values(layer, v_new, part_o, part_p, part_s, cache_v, new_cache_v):
    batch = v_new.shape[0]
    bb = SAMPLE_VALUES_BATCH
    per_seq = lambda a: pl.BlockSpec((bb,) + a.shape[1:], lambda b: (b,) + (0,) * (a.ndim - 1))
    cache_spec = pl.BlockSpec((None, bb) + cache_v.shape[2:], lambda b: (layer, b, 0, 0, 0))
    in_specs = [per_seq(v_new), per_seq(part_o), per_seq(part_p), per_seq(part_s), cache_spec]
    args = [v_new, part_o, part_p, part_s, cache_v]
    aliases = {}
    n_alias = 0
    if new_cache_v is not None:
        n_alias = 1
        in_specs.append(pl.BlockSpec(memory_space=pl.ANY))
        aliases = {len(args): 1}
        args.append(new_cache_v)
    return pl.pallas_call(
        functools.partial(_sample_values_kernel, n_alias),
        out_shape=[jax.ShapeDtypeStruct((batch, 1, GROUP_WIDTH), F32),
                   jax.ShapeDtypeStruct(cache_v.shape, F32)],
        grid=(batch // bb,), in_specs=in_specs,
        out_specs=[pl.BlockSpec((bb, 1, GROUP_WIDTH), lambda b: (b, 0, 0)), cache_spec],
        input_output_aliases=aliases, name="attn_sample_values",
        compiler_params=pltpu.CompilerParams(
            dimension_semantics=("arbitrary",), vmem_limit_bytes=V7X_VMEM_LIMIT),
    )(*args)


def _rope_tables(pos):
    half = HEAD_DIM // 2
    inv = np.power(ROPE_THETA, -2.0 * np.arange(half, dtype=np.float64) / HEAD_DIM)
    ang = np.asarray(pos, dtype=np.float64)[:, None] * inv[None, :]
    cos, sin = np.cos(ang), np.sin(ang)
    reps = V7X_LANES // HEAD_DIM
    return (np.concatenate([cos, cos] * reps, axis=1).astype(np.float32),
            np.concatenate([-sin, sin] * reps, axis=1).astype(np.float32))


def kernel(x_prompt, x_sample, p_prompt, p_sample, cache_k_w128, cache_v_w128, cache_k_w512, cache_v_w512, cache_k_w2048, cache_v_w2048, state_conv, w_in, conv_w, w_attn_out, w_conv_out, w_o, g_pre_mix, g_post_mix, w_up, w_down, g_pre_mlp, g_post_mlp, g_ple, w_ple_gate, w_ple_proj):
    n_seq, seq_len, _ = x_prompt.shape
    n_dec = x_sample.shape[0]
    rows = n_seq * seq_len

    w_in_b = w_in[0].astype(BF16)
    w_attn_out_b = w_attn_out.astype(BF16)
    w_ple_proj_b = w_ple_proj.astype(BF16)
    gains = jnp.stack([g_pre_mix, g_post_mix, g_pre_mlp, g_post_mlp, g_ple], axis=1)
    g_in = g_pre_mix[:, None, :]

    cs_p, sn_p = _rope_tables(np.arange(seq_len))
    cs_s, sn_s = _rope_tables(np.full((n_dec,), PAST_LEN))

    caches = (cache_k_w128, cache_v_w128, cache_k_w512, cache_v_w512, cache_k_w2048, cache_v_w2048)
    caches_t = [jnp.transpose(c, (0, 1, 3, 4, 2)) for c in caches]

    hp = x_prompt.reshape(rows, D_MODEL)
    hs = x_sample.reshape(n_dec, D_MODEL)
    ple_p = p_prompt.reshape(DEPTH, rows, PLE_DIM)
    ple_s = p_sample.reshape(DEPTH, n_dec, PLE_DIM)

    conv_p, conv_s = [], []
    states_p = None
    key_caches = None
    value_cache = None
    n_states = 2 * N_GROUPS
    as_rows = lambda t: jnp.transpose(t, (1, 0, 2)).reshape(n_dec, 1, ATTN_WIDTH)
    for i in range(DEPTH):
        prev = (state_conv[i, :, 0, :], state_conv[i, :, 1, :])
        qs, ks, vs, cbcs, sgs, us = _inproj(i, hs, cs_s, sn_s, g_in, w_in_b, conv_w, prev=prev)
        vs_rows = as_rows(vs)
        side = (as_rows(qs), as_rows(ks), vs_rows, caches_t[:N_KEY_SIDE_CACHES], key_caches)
        res = _inproj(i, hp, cs_p, sn_p, g_in, w_in_b, conv_w, seq_len=seq_len, states=states_p,
                      side=side)
        q, k, v, cbc, sg, cst = res[:6]
        states_p = res[6:6 + n_states]
        key_caches = res[6 + n_states:6 + n_states + N_KEY_SIDE_CACHES]
        part_o, part_p, part_s = res[6 + n_states + N_KEY_SIDE_CACHES:]
        casts = ([(w_up, i), (w_o, i), (w_ple_gate, i), (w_conv_out, i)], [(w_down, i)],
                 [(w_in, i + 1)] if i + 1 < DEPTH else [])
        res16 = _attn_group(2, q, k, v, n_seq, seq_len, casts[2])
        res4 = _attn_group(1, q, k, v, n_seq, seq_len, casts[1])
        res1 = _attn_group(0, q, k, v, n_seq, seq_len, casts[0], others=res4[:2] + res16[:2])
        attn_p, w_up_b, w_o_b, w_ple_gate_b, w_conv_out_b = res1
        w_down_b = res4[2]
        w_in_next = res16[2] if len(res16) > 2 else None
        tail_w = [w_attn_out_b, w_conv_out_b, w_o_b, w_up_b, w_down_b, w_ple_gate_b, w_ple_proj_b]
        hp = _tail(i, hp, attn_p, cbc, sg, ple_p, tail_w, gains, tm=TAIL_ROW_TILE, parts=ROW_PARTS)
        conv_p.append(cst[:, V7X_SUBLANES - (CONV_WIDTH - 1):, :])

        attn_s, value_cache = _sample_values(i, vs_rows, part_o, part_p, part_s, caches_t[-1], value_cache)
        hs = _tail(i, hs, attn_s.reshape(n_dec, GROUP_WIDTH), cbcs, sgs, ple_s, tail_w, gains,
                   tm=n_dec, parts=1)
        conv_s.append(jnp.stack([prev[1], us], axis=1))
        w_in_b = w_in_next

    to_rows = lambda c: jnp.transpose(c, (0, 1, 4, 2, 3))
    return (hp.reshape(n_seq, seq_len, D_MODEL), hs.reshape(n_dec, 1, D_MODEL),
            *[to_rows(c) for c in states_p], jnp.stack(conv_p),
            *[to_rows(c) for c in (*key_caches, value_cache)], jnp.stack(conv_s))
```
